```python
import math
import jax, jax.numpy as jnp
from jax import lax
import numpy as np

D_MODEL = 1024
BATCH = 2
SEQ = 8192
DEPTH = 2

N_MIXERS = 2
N_HEADS = 16
N_KV_HEADS = 4
HEAD_DIM = D_MODEL // N_HEADS
GROUP = N_HEADS // N_KV_HEADS
ROT_DIM = HEAD_DIM // 4
ROPE_THETA = 500000.0
WINDOW = 128
BLK = 128
QKV_DIM = (N_HEADS + 2 * N_KV_HEADS) * HEAD_DIM
CONV_CH = D_MODEL
CONV_WIDTH = 31
CONV_PAD = (CONV_WIDTH - 1) // 2
D_FF = ((8 * D_MODEL // 3 + 255) // 256) * 256
N_ATTN_LAYERS = (DEPTH + 1) // 2
N_CONV_LAYERS = DEPTH // 2
EPS = 1e-6
NEG = -1e30

kernel_name = "hybrid_window_gqa_conformer_conv_encoder"


def rmsnorm(x, g):
    xf = x.astype(jnp.float32)
    y = xf * lax.rsqrt(jnp.mean(xf * xf, axis=-1, keepdims=True) + EPS)
    return (y * g.astype(jnp.float32)).astype(x.dtype)


def layernorm(x, g, b):
    xf = x.astype(jnp.float32)
    mu = jnp.mean(xf, axis=-1, keepdims=True)
    var = jnp.mean(jnp.square(xf - mu), axis=-1, keepdims=True)
    y = (xf - mu) * lax.rsqrt(var + EPS)
    return (y * g.astype(jnp.float32) + b.astype(jnp.float32)).astype(x.dtype)


def partial_rope(x, cos, sin):
    half = ROT_DIM // 2
    x1 = x[..., :half].astype(jnp.float32)
    x2 = x[..., half:ROT_DIM].astype(jnp.float32)
    rot = jnp.concatenate([x1 * cos - x2 * sin, x2 * cos + x1 * sin], axis=-1)
    return jnp.concatenate([rot.astype(x.dtype), x[..., ROT_DIM:]], axis=-1)


def window_attention(h, w_qkv, w_o, sink):
    B, S, _ = h.shape
    nb = S // BLK
    qkv = h @ w_qkv
    q = qkv[..., :N_HEADS * HEAD_DIM].reshape(B, S, N_HEADS, HEAD_DIM)
    k = qkv[..., N_HEADS * HEAD_DIM:(N_HEADS + N_KV_HEADS) * HEAD_DIM].reshape(B, S, N_KV_HEADS, HEAD_DIM)
    v = qkv[..., (N_HEADS + N_KV_HEADS) * HEAD_DIM:].reshape(B, S, N_KV_HEADS, HEAD_DIM)

    pos = jnp.arange(S, dtype=jnp.float32)
    inv_freq = ROPE_THETA ** (-jnp.arange(0, ROT_DIM, 2, dtype=jnp.float32) / ROT_DIM)
    ang = pos[:, None] * inv_freq[None, :]
    cos = jnp.cos(ang)[:, None, :]
    sin = jnp.sin(ang)[:, None, :]
    q = partial_rope(q, cos, sin)
    k = partial_rope(k, cos, sin)

    qb = q.reshape(B, nb, BLK, N_KV_HEADS, GROUP, HEAD_DIM)
    pad = ((0, 0), (BLK, BLK), (0, 0), (0, 0))
    kp = jnp.pad(k, pad).reshape(B, nb + 2, BLK, N_KV_HEADS, HEAD_DIM)
    vp = jnp.pad(v, pad).reshape(B, nb + 2, BLK, N_KV_HEADS, HEAD_DIM)
    kb = jnp.concatenate([kp[:, :-2], kp[:, 1:-1], kp[:, 2:]], axis=2)
    vb = jnp.concatenate([vp[:, :-2], vp[:, 1:-1], vp[:, 2:]], axis=2)

    scale = 1.0 / math.sqrt(HEAD_DIM)
    s = jnp.einsum('bnqkgd,bnckd->bnkgqc', qb, kb).astype(jnp.float32) * scale

    qi = jnp.arange(BLK)
    ci = jnp.arange(3 * BLK)
    rel = ci[None, :] - BLK - qi[:, None]
    in_window = jnp.abs(rel) <= WINDOW
    key_pos = jnp.arange(nb)[:, None] * BLK - BLK + ci[None, :]
    in_range = (key_pos >= 0) & (key_pos < S)
    valid = in_window[None, :, :] & in_range[:, None, :]
    s = jnp.where(valid[None, :, None, None, :, :], s, NEG)

    sink_l = sink.astype(jnp.float32).reshape(1, 1, N_KV_HEADS, GROUP, 1, 1)
    m = jnp.maximum(jnp.max(s, axis=-1, keepdims=True), sink_l)
    e = jnp.exp(s - m)
    p = e / (jnp.sum(e, axis=-1, keepdims=True) + jnp.exp(sink_l - m))

    o = jnp.einsum('bnkgqc,bnckd->bnqkgd', p.astype(vb.dtype), vb)
    o = o.reshape(B, S, N_HEADS * HEAD_DIM)
    return o @ w_o


def conformer_conv(h, w_pw1, b_pw1, w_dw, b_dw, ln_g, ln_b, w_pw2, b_pw2):
    u = h @ w_pw1 + b_pw1
    a, gate = u[..., :CONV_CH], u[..., CONV_CH:]
    u = a * jax.nn.sigmoid(gate)
    u = lax.conv_general_dilated(
        u, w_dw[:, None, :].astype(u.dtype), window_strides=(1,),
        padding=[(CONV_PAD, CONV_PAD)],
        dimension_numbers=('NWC', 'WIO', 'NWC'),
        feature_group_count=CONV_CH) + b_dw
    u = layernorm(u, ln_g, ln_b)
    u = jax.nn.silu(u)
    return u @ w_pw2 + b_pw2


def swiglu_ffn(h, w_gu, w_down):
    gu = h @ w_gu
    return (jax.nn.silu(gu[..., :D_FF]) * gu[..., D_FF:]) @ w_down


def setup_inputs(seed: int = 0) -> dict:
    key = jax.random.key(seed)
    ks = jax.random.split(key, 24)
    f32 = jnp.float32
    nrm = lambda k, shp, sc: jax.random.normal(k, shp, f32) * sc
    gain = lambda k, shp: 1.0 + 0.02 * jax.random.normal(k, shp, f32)
    na, nc = N_ATTN_LAYERS, N_CONV_LAYERS
    return {
        "x": nrm(ks[0], (BATCH, SEQ, D_MODEL), 1.0),
        "attn_norm": gain(ks[1], (na, D_MODEL)),
        "attn_w_qkv": nrm(ks[2], (na, D_MODEL, QKV_DIM), D_MODEL ** -0.5),
        "attn_w_o": nrm(ks[3], (na, N_HEADS * HEAD_DIM, D_MODEL), (N_HEADS * HEAD_DIM) ** -0.5),
        "attn_sink": nrm(ks[4], (na, N_HEADS), 0.5),
        "conv_norm": gain(ks[5], (nc, D_MODEL)),
        "conv_w_pw1": nrm(ks[6], (nc, D_MODEL, 2 * CONV_CH), D_MODEL ** -0.5),
        "conv_b_pw1": nrm(ks[7], (nc, 2 * CONV_CH), 0.02),
        "conv_w_dw": nrm(ks[8], (nc, CONV_WIDTH, CONV_CH), CONV_WIDTH ** -0.5),
        "conv_b_dw": nrm(ks[9], (nc, CONV_CH), 0.02),
        "conv_ln_g": gain(ks[10], (nc, CONV_CH)),
        "conv_ln_b": nrm(ks[11], (nc, CONV_CH), 0.02),
        "conv_w_pw2": nrm(ks[12], (nc, CONV_CH, D_MODEL), CONV_CH ** -0.5),
        "conv_b_pw2": nrm(ks[13], (nc, D_MODEL), 0.02),
        "ffn_norm": gain(ks[14], (DEPTH, D_MODEL)),
        "ffn_w_gu": nrm(ks[15], (DEPTH, D_MODEL, 2 * D_FF), D_MODEL ** -0.5),
        "ffn_w_down": nrm(ks[16], (DEPTH, D_FF, D_MODEL), D_FF ** -0.5),
        "final_norm": gain(ks[17], (D_MODEL,)),
    }


def reference(x, attn_norm, attn_w_qkv, attn_w_o, attn_sink,
              conv_norm, conv_w_pw1, conv_b_pw1, conv_w_dw, conv_b_dw,
              conv_ln_g, conv_ln_b, conv_w_pw2, conv_b_pw2,
              ffn_norm, ffn_w_gu, ffn_w_down, final_norm):
    for i in range(DEPTH):
        j = i // N_MIXERS
        if i % N_MIXERS == 0:
            h = rmsnorm(x, attn_norm[j])
            x = x + window_attention(h, attn_w_qkv[j], attn_w_o[j], attn_sink[j])
        else:
            h = rmsnorm(x, conv_norm[j])
            x = x + conformer_conv(h, conv_w_pw1[j], conv_b_pw1[j], conv_w_dw[j],
                                   conv_b_dw[j], conv_ln_g[j], conv_ln_b[j],
                                   conv_w_pw2[j], conv_b_pw2[j])
        h = rmsnorm(x, ffn_norm[i])
        x = x + swiglu_ffn(h, ffn_w_gu[i], ffn_w_down[i])
    return rmsnorm(x, final_norm)
```

```python
import functools
import math

import jax
import jax.numpy as jnp
from jax import lax
from jax.experimental import pallas as pl
from jax.experimental.pallas import tpu as pltpu

D_MODEL = 1024
N_HEADS = 16
N_KV_HEADS = 4
HEAD_DIM = D_MODEL // N_HEADS
GROUP = N_HEADS // N_KV_HEADS
ROT_DIM = HEAD_DIM // 4
ROT_HALF = ROT_DIM // 2
ROPE_THETA = 500000.0
WINDOW = 128
BLK = 128
Q_DIM = N_HEADS * HEAD_DIM
KV_DIM = N_KV_HEADS * HEAD_DIM
QKV_DIM = Q_DIM + 2 * KV_DIM
CONV_CH = D_MODEL
CONV_WIDTH = 31
CONV_PAD = (CONV_WIDTH - 1) // 2
D_FF = ((8 * D_MODEL // 3 + 255) // 256) * 256
EPS = 1e-6
NEG = -1e30

LANES = 128
SUBLANES = 8
VMEM_LIMIT_BYTES = 56 * 1024 * 1024

TOK_TILE = 512
FF_CHUNK = 256
HALO = 2 * SUBLANES
CONV_ROWS = 128
N_LANE_TILES = D_MODEL // LANES
HEADS_PER_TILE = LANES // HEAD_DIM

_f32 = jnp.float32
_bf16 = jnp.bfloat16


def _rmsnorm(xf, g):
    y = xf * lax.rsqrt(jnp.mean(xf * xf, axis=-1, keepdims=True) + EPS)
    return y * g


def _const_spec(shape):
    nd = len(shape)
    return pl.BlockSpec(shape, lambda *_: (0,) * nd, pipeline_mode=pl.Buffered(1))


def _compiler_params(n_axes):
    return pltpu.CompilerParams(
        dimension_semantics=("arbitrary",) * n_axes,
        vmem_limit_bytes=VMEM_LIMIT_BYTES,
    )


def _qkv_kernel(x_ref, g_ref, w_ref, cos_ref, sin_hi_ref, sin_lo_ref, q_ref, k_ref, v_ref):
    h = _rmsnorm(x_ref[...], g_ref[...]).astype(_bf16)
    qkv = jnp.dot(h, w_ref[...], preferred_element_type=_f32)
    cos = cos_ref[...]
    sin_hi = sin_hi_ref[...]
    sin_lo = sin_lo_ref[...]
    scale = 1.0 / math.sqrt(HEAD_DIM)

    def rope(t):
        return (t * cos + pltpu.roll(t, ROT_HALF, 1) * sin_hi
                + pltpu.roll(t, LANES - ROT_HALF, 1) * sin_lo)

    for j in range(Q_DIM // LANES):
        t = qkv[:, j * LANES:(j + 1) * LANES]
        q_ref[:, j * LANES:(j + 1) * LANES] = (rope(t) * scale).astype(_bf16)
    for j in range(KV_DIM // LANES):
        t = qkv[:, Q_DIM + j * LANES:Q_DIM + (j + 1) * LANES]
        k_ref[:, j * LANES:(j + 1) * LANES] = rope(t).astype(_bf16)
    v_ref[...] = qkv[:, Q_DIM + KV_DIM:].astype(_bf16)


def _qkv_call(x2, g, w_qkv, cos_t, sin_hi_t, sin_lo_t, seq):
    m = x2.shape[0]
    tiles_per_seq = seq // TOK_TILE
    tok = lambda i: (i, 0)
    pos = lambda i: (i % tiles_per_seq, 0)
    return pl.pallas_call(
        _qkv_kernel,
        grid=(m // TOK_TILE,),
        in_specs=[
            pl.BlockSpec((TOK_TILE, D_MODEL), tok),
            _const_spec((1, D_MODEL)),
            _const_spec((D_MODEL, QKV_DIM)),
            pl.BlockSpec((TOK_TILE, LANES), pos),
            pl.BlockSpec((TOK_TILE, LANES), pos),
            pl.BlockSpec((TOK_TILE, LANES), pos),
        ],
        out_specs=[
            pl.BlockSpec((TOK_TILE, Q_DIM), tok),
            pl.BlockSpec((TOK_TILE, KV_DIM), tok),
            pl.BlockSpec((TOK_TILE, KV_DIM), tok),
        ],
        out_shape=[
            jax.ShapeDtypeStruct((m, Q_DIM), _bf16),
            jax.ShapeDtypeStruct((m, KV_DIM), _bf16),
            jax.ShapeDtypeStruct((m, KV_DIM), _bf16),
        ],
        compiler_params=_compiler_params(1),
        name="qkv_rope",
    )(x2, g, w_qkv, cos_t, sin_hi_t, sin_lo_t)


def _attn_kernel(sink_ref, q_ref, kp_ref, kc_ref, kn_ref, vp_ref, vc_ref, vn_ref, o_ref):
    n = pl.program_id(1)
    nb = pl.num_programs(1)
    keys = 3 * BLK

    qi = lax.broadcasted_iota(jnp.int32, (BLK, keys), 0)
    ci = lax.broadcasted_iota(jnp.int32, (BLK, keys), 1)
    rel = ci - BLK - qi
    in_window = jnp.abs(rel) <= WINDOW
    in_range = ((ci >= BLK) | (n > 0)) & ((ci < 2 * BLK) | (n < nb - 1))
    valid = in_window & in_range

    lane = lax.broadcasted_iota(jnp.int32, (keys, LANES), 1)
    lo_half = lane < HEAD_DIM
    out_lo_half = lax.broadcasted_iota(jnp.int32, (BLK, LANES), 1) < HEAD_DIM
    zero = jnp.zeros((), _bf16)

    for g in range(N_KV_HEADS):
        tile = g // HEADS_PER_TILE
        sl = slice(tile * LANES, (tile + 1) * LANES)
        k_cat = jnp.concatenate([kp_ref[:, sl], kc_ref[:, sl], kn_ref[:, sl]], axis=0)
        v_cat = jnp.concatenate([vp_ref[:, sl], vc_ref[:, sl], vn_ref[:, sl]], axis=0)
        own = lo_half if g % HEADS_PER_TILE == 0 else ~lo_half
        k_own = jnp.where(own, k_cat, zero)
        v_own = jnp.where(own, v_cat, zero)
        k_swap = pltpu.roll(k_own, HEAD_DIM, 1)
        v_swap = pltpu.roll(v_own, HEAD_DIM, 1)
        if g % HEADS_PER_TILE == 0:
            k_lo, k_hi, v_lo, v_hi = k_own, k_swap, v_own, v_swap
        else:
            k_lo, k_hi, v_lo, v_hi = k_swap, k_own, v_swap, v_own
        k_both = jnp.concatenate([k_lo, k_hi], axis=0)
        v_both = jnp.concatenate([v_lo, v_hi], axis=0)

        for pair in range(GROUP // HEADS_PER_TILE):
            qt = g * (GROUP // HEADS_PER_TILE) + pair
            q2 = q_ref[:, qt * LANES:(qt + 1) * LANES]
            s = lax.dot_general(q2, k_both, (((1,), (1,)), ((), ())),
                                preferred_element_type=_f32)
            probs = []
            inv = []
            for half in range(HEADS_PER_TILE):
                head = qt * HEADS_PER_TILE + half
                sink = sink_ref[head]
                sh = jnp.where(valid, s[:, half * keys:(half + 1) * keys], NEG)
                mx = jnp.maximum(jnp.max(sh, axis=-1, keepdims=True), sink)
                e = jnp.exp(sh - mx)
                denom = jnp.sum(e, axis=-1, keepdims=True) + jnp.exp(sink - mx)
                probs.append(e.astype(_bf16))
                inv.append(1.0 / denom)
            p = jnp.concatenate(probs, axis=1)
            o2 = jnp.dot(p, v_both, preferred_element_type=_f32)
            o2 = o2 * jnp.where(out_lo_half, inv[0], inv[1])
            o_ref[:, qt * LANES:(qt + 1) * LANES] = o2.astype(_bf16)


def _attn_call(sink, q3, k3, v3):
    b, s, _ = q3.shape
    nb = s // BLK
    own = lambda bi, n: (bi, n, 0)
    prev = lambda bi, n: (bi, jnp.maximum(n - 1, 0), 0)
    nxt = lambda bi, n: (bi, jnp.minimum(n + 1, nb - 1), 0)
    kv_spec = lambda imap: pl.BlockSpec((None, BLK, KV_DIM), imap)
    return pl.pallas_call(
        _attn_kernel,
        grid=(b, nb),
        in_specs=[
            pl.BlockSpec(memory_space=pltpu.SMEM),
            pl.BlockSpec((None, BLK, Q_DIM), own),
            kv_spec(prev), kv_spec(own), kv_spec(nxt),
            kv_spec(prev), kv_spec(own), kv_spec(nxt),
        ],
        out_specs=pl.BlockSpec((None, BLK, Q_DIM), own),
        out_shape=jax.ShapeDtypeStruct((b, s, Q_DIM), _bf16),
        compiler_params=_compiler_params(2),
        name="window_attn",
    )(sink, q3, k3, k3, k3, v3, v3, v3)


def _ffn_residual(xf, g, wg_ref, wu_ref, wd_ref, act_ref):
    h = _rmsnorm(xf, g).astype(_bf16)
    for c in range(D_FF // FF_CHUNK):
        cols = slice(c * FF_CHUNK, (c + 1) * FF_CHUNK)
        gate = jnp.dot(h, wg_ref[:, cols], preferred_element_type=_f32)
        up = jnp.dot(h, wu_ref[:, cols], preferred_element_type=_f32)
        act_ref[:, cols] = (gate * jax.nn.sigmoid(gate) * up).astype(_bf16)
    return xf + jnp.dot(act_ref[...], wd_ref[...], preferred_element_type=_f32)


def _proj_ffn_kernel(x_ref, o_ref, wo_ref, g_ref, wg_ref, wu_ref, wd_ref, out_ref, act_ref):
    x1 = x_ref[...] + jnp.dot(o_ref[...], wo_ref[...], preferred_element_type=_f32)
    out_ref[...] = _ffn_residual(x1, g_ref[...], wg_ref, wu_ref, wd_ref, act_ref)


def _proj_ffn_call(x2, o2, w_o, g, w_g, w_u, w_d):
    m = x2.shape[0]
    tok = lambda i: (i, 0)
    return pl.pallas_call(
        _proj_ffn_kernel,
        grid=(m // TOK_TILE,),
        in_specs=[
            pl.BlockSpec((TOK_TILE, D_MODEL), tok),
            pl.BlockSpec((TOK_TILE, Q_DIM), tok),
            _const_spec((Q_DIM, D_MODEL)),
            _const_spec((1, D_MODEL)),
            _const_spec((D_MODEL, D_FF)),
            _const_spec((D_MODEL, D_FF)),
            _const_spec((D_FF, D_MODEL)),
        ],
        out_specs=pl.BlockSpec((TOK_TILE, D_MODEL), tok),
        out_shape=jax.ShapeDtypeStruct((m, D_MODEL), _f32),
        scratch_shapes=[pltpu.VMEM((TOK_TILE, D_FF), _bf16)],
        compiler_params=_compiler_params(1),
        name="proj_ffn",
    )(x2, o2, w_o, g, w_g, w_u, w_d)


def _conv_ffn_kernel(x_ref, xp_ref, xn_ref, gc_ref, w1_ref, b1_ref, wdw_ref, bdw_ref, lng_ref, lnb_ref,
                     w2_ref, b2_ref, gf_ref, wg_ref, wu_ref, wd_ref, gfin_ref, out_ref,
                     u_ref, c_ref, act_ref, *, tiles_per_seq):
    i = pl.program_id(0)
    first = (i % tiles_per_seq) == 0
    last = (i % tiles_per_seq) == tiles_per_seq - 1
    rows = TOK_TILE + 2 * HALO

    xf = x_ref[...]
    xh = jnp.concatenate([xp_ref[...], xf, xn_ref[...]], axis=0)
    h = _rmsnorm(xh, gc_ref[...]).astype(_bf16)
    u = jnp.dot(h, w1_ref[...], preferred_element_type=_f32) + b1_ref[...]
    u = u[:, :CONV_CH] * jax.nn.sigmoid(u[:, CONV_CH:])
    r = lax.broadcasted_iota(jnp.int32, (rows, 1), 0)
    outside = ((r < HALO) & first) | ((r >= HALO + TOK_TILE) & last)
    u = jnp.where(outside, 0.0, u)
    for t in range(N_LANE_TILES):
        u_ref[t] = u[:, t * LANES:(t + 1) * LANES]

    def conv_tile(t, carry):
        w = wdw_ref[t]
        bias = bdw_ref[t]
        for rc in range(TOK_TILE // CONV_ROWS):
            acc = jnp.zeros((CONV_ROWS, LANES), _f32)
            for j in range(CONV_WIDTH):
                start = rc * CONV_ROWS + HALO - CONV_PAD + j
                acc = acc + u_ref[t, pl.ds(start, CONV_ROWS), :] * w[j:j + 1, :]
            c_ref[t, pl.ds(rc * CONV_ROWS, CONV_ROWS), :] = acc + bias
        return carry

    lax.fori_loop(0, N_LANE_TILES, conv_tile, 0)

    c = jnp.concatenate([c_ref[t] for t in range(N_LANE_TILES)], axis=1)
    mu = jnp.mean(c, axis=-1, keepdims=True)
    var = jnp.mean(jnp.square(c - mu), axis=-1, keepdims=True)
    y = (c - mu) * lax.rsqrt(var + EPS) * lng_ref[...] + lnb_ref[...]
    y = (y * jax.nn.sigmoid(y)).astype(_bf16)
    x1 = xf + jnp.dot(y, w2_ref[...], preferred_element_type=_f32) + b2_ref[...]
    x2 = _ffn_residual(x1, gf_ref[...], wg_ref, wu_ref, wd_ref, act_ref)
    out_ref[...] = _rmsnorm(x2, gfin_ref[...])


def _conv_ffn_call(x2, gc, w1, b1, wdw, bdw, lng, lnb, w2, b2, gf, w_g, w_u, w_d, gfin, seq):
    m = x2.shape[0]
    tiles_per_seq = seq // TOK_TILE
    halo_per_tile = TOK_TILE // HALO
    n_halo = m // HALO
    tok = lambda i: (i, 0)
    prev = lambda i: (jnp.maximum(i * halo_per_tile - 1, 0), 0)
    nxt = lambda i: (jnp.minimum((i + 1) * halo_per_tile, n_halo - 1), 0)
    rows = TOK_TILE + 2 * HALO
    return pl.pallas_call(
        functools.partial(_conv_ffn_kernel, tiles_per_seq=tiles_per_seq),
        grid=(m // TOK_TILE,),
        in_specs=[
            pl.BlockSpec((TOK_TILE, D_MODEL), tok),
            pl.BlockSpec((HALO, D_MODEL), prev),
            pl.BlockSpec((HALO, D_MODEL), nxt),
            _const_spec((1, D_MODEL)),
            _const_spec((D_MODEL, 2 * CONV_CH)),
            _const_spec((1, 2 * CONV_CH)),
            _const_spec((N_LANE_TILES, CONV_WIDTH, LANES)),
            _const_spec((N_LANE_TILES, 1, LANES)),
            _const_spec((1, CONV_CH)),
            _const_spec((1, CONV_CH)),
            _const_spec((CONV_CH, D_MODEL)),
            _const_spec((1, D_MODEL)),
            _const_spec((1, D_MODEL)),
            _const_spec((D_MODEL, D_FF)),
            _const_spec((D_MODEL, D_FF)),
            _const_spec((D_FF, D_MODEL)),
            _const_spec((1, D_MODEL)),
        ],
        out_specs=pl.BlockSpec((TOK_TILE, D_MODEL), tok),
        out_shape=jax.ShapeDtypeStruct((m, D_MODEL), _f32),
        scratch_shapes=[
            pltpu.VMEM((N_LANE_TILES, rows, LANES), _f32),
            pltpu.VMEM((N_LANE_TILES, TOK_TILE, LANES), _f32),
            pltpu.VMEM((TOK_TILE, D_FF), _bf16),
        ],
        compiler_params=_compiler_params(1),
        name="conv_ffn",
    )(x2, x2, x2, gc, w1, b1, wdw, bdw, lng, lnb, w2, b2, gf, w_g, w_u, w_d, gfin)


def _rope_tables(seq):
    pos = jnp.arange(seq, dtype=_f32)
    inv_freq = ROPE_THETA ** (-jnp.arange(0, ROT_DIM, 2, dtype=_f32) / ROT_DIM)
    ang = pos[:, None] * inv_freq[None, :]
    cos = jnp.cos(ang)
    sin = jnp.sin(ang)
    ones = jnp.ones((seq, HEAD_DIM - ROT_DIM), _f32)
    zeros_half = jnp.zeros((seq, ROT_HALF), _f32)
    zeros_rest = jnp.zeros((seq, HEAD_DIM - ROT_DIM), _f32)
    cos_h = jnp.concatenate([cos, cos, ones], axis=1)
    sin_hi_h = jnp.concatenate([zeros_half, sin, zeros_rest], axis=1)
    sin_lo_h = jnp.concatenate([-sin, zeros_half, zeros_rest], axis=1)
    rep = lambda a: jnp.tile(a, (1, HEADS_PER_TILE))
    return rep(cos_h), rep(sin_hi_h), rep(sin_lo_h)


def kernel(x, attn_norm, attn_w_qkv, attn_w_o, attn_sink, conv_norm, conv_w_pw1, conv_b_pw1, conv_w_dw,
           conv_b_dw, conv_ln_g, conv_ln_b, conv_w_pw2, conv_b_pw2, ffn_norm, ffn_w_gu, ffn_w_down, final_norm):
    b, s, d = x.shape
    assert d == D_MODEL and s % TOK_TILE == 0 and s % BLK == 0
    assert attn_norm.shape[0] == 1 and conv_norm.shape[0] == 1 and ffn_norm.shape[0] == 2
    m = b * s
    x2 = x.reshape(m, d)
    row = lambda v: v.reshape(1, -1).astype(_f32)
    bf = lambda w: w.astype(_bf16)

    cos_t, sin_hi_t, sin_lo_t = _rope_tables(s)
    q, k, v = _qkv_call(x2, row(attn_norm[0]), bf(attn_w_qkv[0]), cos_t, sin_hi_t, sin_lo_t, s)
    o = _attn_call(attn_sink[0].astype(_f32), q.reshape(b, s, Q_DIM), k.reshape(b, s, KV_DIM),
                   v.reshape(b, s, KV_DIM))
    w_gu0 = bf(ffn_w_gu[0])
    x2 = _proj_ffn_call(x2, o.reshape(m, Q_DIM), bf(attn_w_o[0]), row(ffn_norm[0]),
                        w_gu0[:, :D_FF], w_gu0[:, D_FF:], bf(ffn_w_down[0]))

    w_gu1 = bf(ffn_w_gu[1])
    wdw = conv_w_dw[0].reshape(CONV_WIDTH, N_LANE_TILES, LANES).transpose(1, 0, 2)
    bdw = conv_b_dw[0].reshape(N_LANE_TILES, 1, LANES)
    out = _conv_ffn_call(x2, row(conv_norm[0]), bf(conv_w_pw1[0]), row(conv_b_pw1[0]), wdw, bdw,
                         row(conv_ln_g[0]), row(conv_ln_b[0]), bf(conv_w_pw2[0]), row(conv_b_pw2[0]),
                         row(ffn_norm[1]), w_gu1[:, :D_FF], w_gu1[:, D_FF:], bf(ffn_w_down[1]),
                         row(final_norm), s)
    return out.reshape(b, s, d)
```

```python
import functools
import math

import jax
import jax.numpy as jnp
from jax import lax
from jax.experimental import pallas as pl
from jax.experimental.pallas import tpu as pltpu

D_MODEL = 1024
N_HEADS = 16
N_KV_HEADS = 4
HEAD_DIM = D_MODEL // N_HEADS
GROUP = N_HEADS // N_KV_HEADS
ROT_DIM = HEAD_DIM // 4
ROT_HALF = ROT_DIM // 2
ROPE_THETA = 500000.0
WINDOW = 128
BLK = 128
Q_DIM = N_HEADS * HEAD_DIM
KV_DIM = N_KV_HEADS * HEAD_DIM
QKV_DIM = Q_DIM + 2 * KV_DIM
CONV_CH = D_MODEL
CONV_WIDTH = 31
CONV_PAD = (CONV_WIDTH - 1) // 2
D_FF = ((8 * D_MODEL // 3 + 255) // 256) * 256
EPS = 1e-6
NEG = -1e30

LANES = 128
SUBLANES = 8
VMEM_LIMIT_BYTES = 56 * 1024 * 1024

TOK_TILE = 512
FF_CHUNK = 256
HALO = 2 * SUBLANES
CONV_ROWS = 128
N_LANE_TILES = D_MODEL // LANES
HEADS_PER_TILE = LANES // HEAD_DIM
PAIRS_PER_GROUP = GROUP // HEADS_PER_TILE
ATT_QBLKS = 4
ATT_KBLKS = ATT_QBLKS + 2
PAIR_ROWS = HEADS_PER_TILE * BLK

_f32 = jnp.float32
_bf16 = jnp.bfloat16


def _rmsnorm(xf, g):
    y = xf * lax.rsqrt(jnp.mean(xf * xf, axis=-1, keepdims=True) + EPS)
    return y * g


def _const_spec(shape):
    nd = len(shape)
    return pl.BlockSpec(shape, lambda *_: (0,) * nd, pipeline_mode=pl.Buffered(1))


def _compiler_params(n_axes):
    return pltpu.CompilerParams(
        dimension_semantics=("arbitrary",) * n_axes,
        vmem_limit_bytes=VMEM_LIMIT_BYTES,
    )


def _qkv_kernel(x_ref, g_ref, w_ref, cos_ref, sin_hi_ref, sin_lo_ref, q_ref, k_ref, v_ref):
    h = _rmsnorm(x_ref[...], g_ref[...]).astype(_bf16)
    qkv = jnp.dot(h, w_ref[...], preferred_element_type=_f32)
    cos = cos_ref[...]
    sin_hi = sin_hi_ref[...]
    sin_lo = sin_lo_ref[...]
    scale = 1.0 / math.sqrt(HEAD_DIM)

    def rope(t):
        return (t * cos + pltpu.roll(t, ROT_HALF, 1) * sin_hi
                + pltpu.roll(t, LANES - ROT_HALF, 1) * sin_lo)

    for j in range(Q_DIM // LANES):
        t = qkv[:, j * LANES:(j + 1) * LANES]
        q_ref[:, j * LANES:(j + 1) * LANES] = (rope(t) * scale).astype(_bf16)
    for j in range(KV_DIM // LANES):
        t = qkv[:, Q_DIM + j * LANES:Q_DIM + (j + 1) * LANES]
        k_ref[:, j * LANES:(j + 1) * LANES] = rope(t).astype(_bf16)
    v_ref[...] = qkv[:, Q_DIM + KV_DIM:].astype(_bf16)


def _qkv_call(x2, g, w_qkv, cos_t, sin_hi_t, sin_lo_t, seq):
    m = x2.shape[0]
    tiles_per_seq = seq // TOK_TILE
    tok = lambda i: (i, 0)
    pos = lambda i: (i % tiles_per_seq, 0)
    return pl.pallas_call(
        _qkv_kernel,
        grid=(m // TOK_TILE,),
        in_specs=[
            pl.BlockSpec((TOK_TILE, D_MODEL), tok),
            _const_spec((1, D_MODEL)),
            _const_spec((D_MODEL, QKV_DIM)),
            pl.BlockSpec((TOK_TILE, LANES), pos),
            pl.BlockSpec((TOK_TILE, LANES), pos),
            pl.BlockSpec((TOK_TILE, LANES), pos),
        ],
        out_specs=[
            pl.BlockSpec((TOK_TILE, Q_DIM), tok),
            pl.BlockSpec((TOK_TILE, KV_DIM), tok),
            pl.BlockSpec((TOK_TILE, KV_DIM), tok),
        ],
        out_shape=[
            jax.ShapeDtypeStruct((m, Q_DIM), _bf16),
            jax.ShapeDtypeStruct((m, KV_DIM), _bf16),
            jax.ShapeDtypeStruct((m, KV_DIM), _bf16),
        ],
        compiler_params=_compiler_params(1),
        name="qkv_rope",
    )(x2, g, w_qkv, cos_t, sin_hi_t, sin_lo_t)


def _attn_kernel(sink_ref, q_ref, kp_ref, kc_ref, kn_ref, vp_ref, vc_ref, vn_ref, o_ref, kz_ref, vz_ref):
    step = pl.program_id(1)
    n_steps = pl.num_programs(1)

    lane = lax.broadcasted_iota(jnp.int32, (BLK, LANES), 1)
    low_lanes = lane < HEAD_DIM
    low_lanes_f32 = (lane & HEAD_DIM) == 0
    zero = jnp.zeros((), _bf16)
    high_flag = (lane // HEAD_DIM).astype(_f32)
    ones_low = (1.0 - high_flag).astype(_bf16)
    ones_high = high_flag.astype(_bf16)

    for kb in range(ATT_KBLKS):
        if kb == 0:
            k_blk, v_blk = kp_ref[...], vp_ref[...]
        elif kb == ATT_KBLKS - 1:
            k_blk, v_blk = kn_ref[...], vn_ref[...]
        else:
            rows = slice((kb - 1) * BLK, kb * BLK)
            k_blk, v_blk = kc_ref[rows, :], vc_ref[rows, :]
        for g in range(N_KV_HEADS):
            tile = slice((g // HEADS_PER_TILE) * LANES, (g // HEADS_PER_TILE + 1) * LANES)
            own = low_lanes if g % HEADS_PER_TILE == 0 else ~low_lanes
            k_own = jnp.where(own, k_blk[:, tile], zero)
            v_own = jnp.where(own, v_blk[:, tile], zero)
            k_swap = pltpu.roll(k_own, HEAD_DIM, 1)
            v_swap = pltpu.roll(v_own, HEAD_DIM, 1)
            if g % HEADS_PER_TILE == 0:
                k_lo, k_hi, v_lo, v_hi = k_own, k_swap, v_own, v_swap
            else:
                k_lo, k_hi, v_lo, v_hi = k_swap, k_own, v_swap, v_own
            base = kb * PAIR_ROWS
            kz_ref[g, base:base + BLK, :] = k_lo
            kz_ref[g, base + BLK:base + PAIR_ROWS, :] = k_hi
            vz_ref[g, base:base + BLK, :LANES] = v_lo
            vz_ref[g, base + BLK:base + PAIR_ROWS, :LANES] = v_hi
            vz_ref[g, base:base + BLK, LANES:] = ones_low
            vz_ref[g, base + BLK:base + PAIR_ROWS, LANES:] = ones_high

    qi = lax.broadcasted_iota(jnp.int32, (BLK, BLK), 0)
    ci = lax.broadcasted_iota(jnp.int32, (BLK, BLK), 1)
    prev_in_window = ci >= qi
    next_in_window = ci <= qi

    def q_block(j, carry):
        rows = pl.ds(pl.multiple_of(j * BLK, BLK), BLK)
        stack = pl.ds(pl.multiple_of(j * PAIR_ROWS, PAIR_ROWS), 3 * PAIR_ROWS)
        prev_ok = prev_in_window & ((step > 0) | (j > 0))
        next_ok = next_in_window & ((step < n_steps - 1) | (j < ATT_QBLKS - 1))
        for g in range(N_KV_HEADS):
            for pair in range(PAIRS_PER_GROUP):
                qt = g * PAIRS_PER_GROUP + pair
                q2 = q_ref[rows, qt * LANES:(qt + 1) * LANES]
                s = lax.dot_general(q2, kz_ref[g, stack, :], (((1,), (1,)), ((), ())),
                                    preferred_element_type=_f32)
                probs = [None] * (3 * HEADS_PER_TILE)
                sink_terms = []
                for half in range(HEADS_PER_TILE):
                    sink = sink_ref[qt * HEADS_PER_TILE + half]
                    col = lambda kb: slice((kb * HEADS_PER_TILE + half) * BLK, (kb * HEADS_PER_TILE + half + 1) * BLK)
                    s_prev = jnp.where(prev_ok, s[:, col(0)], NEG)
                    s_own = s[:, col(1)]
                    s_next = jnp.where(next_ok, s[:, col(2)], NEG)
                    mx = jnp.max(jnp.maximum(jnp.maximum(s_prev, s_own), s_next), axis=-1, keepdims=True)
                    mx = jnp.maximum(mx, sink)
                    probs[0 * HEADS_PER_TILE + half] = jnp.exp(s_prev - mx).astype(_bf16)
                    probs[1 * HEADS_PER_TILE + half] = jnp.exp(s_own - mx).astype(_bf16)
                    probs[2 * HEADS_PER_TILE + half] = jnp.exp(s_next - mx).astype(_bf16)
                    sink_terms.append(jnp.exp(sink - mx))
                p = jnp.concatenate(probs, axis=1)
                o_ext = jnp.dot(p, vz_ref[g, stack, :], preferred_element_type=_f32)
                denom = o_ext[:, LANES:] + jnp.where(low_lanes_f32, sink_terms[0], sink_terms[1])
                o_ref[rows, qt * LANES:(qt + 1) * LANES] = (o_ext[:, :LANES] / denom).astype(_bf16)
        return carry

    lax.fori_loop(0, ATT_QBLKS, q_block, 0)


def _attn_call(sink, q3, k3, v3):
    b, s, _ = q3.shape
    nb = s // BLK
    rows = ATT_QBLKS * BLK
    own = lambda bi, n: (bi, n, 0)
    prev = lambda bi, n: (bi, jnp.maximum(n * ATT_QBLKS - 1, 0), 0)
    nxt = lambda bi, n: (bi, jnp.minimum((n + 1) * ATT_QBLKS, nb - 1), 0)
    edge_spec = lambda imap: pl.BlockSpec((None, BLK, KV_DIM), imap)
    own_spec = pl.BlockSpec((None, rows, KV_DIM), own)
    return pl.pallas_call(
        _attn_kernel,
        grid=(b, s // rows),
        in_specs=[
            pl.BlockSpec(memory_space=pltpu.SMEM),
            pl.BlockSpec((None, rows, Q_DIM), own),
            edge_spec(prev), own_spec, edge_spec(nxt),
            edge_spec(prev), own_spec, edge_spec(nxt),
        ],
        out_specs=pl.BlockSpec((None, rows, Q_DIM), own),
        out_shape=jax.ShapeDtypeStruct((b, s, Q_DIM), _bf16),
        scratch_shapes=[
            pltpu.VMEM((N_KV_HEADS, ATT_KBLKS * PAIR_ROWS, LANES), _bf16),
            pltpu.VMEM((N_KV_HEADS, ATT_KBLKS * PAIR_ROWS, 2 * LANES), _bf16),
        ],
        compiler_params=_compiler_params(2),
        name="window_attn",
    )(sink, q3, k3, k3, k3, v3, v3, v3)


def _ffn_residual(xf, g, wgu_ref, wd_ref, act_ref):
    h = _rmsnorm(xf, g).astype(_bf16)
    for c in range(D_FF // FF_CHUNK):
        lo = c * FF_CHUNK
        gate = jnp.dot(h, wgu_ref[:, lo:lo + FF_CHUNK], preferred_element_type=_f32)
        up = jnp.dot(h, wgu_ref[:, D_FF + lo:D_FF + lo + FF_CHUNK], preferred_element_type=_f32)
        act_ref[:, lo:lo + FF_CHUNK] = (gate * jax.nn.sigmoid(gate) * up).astype(_bf16)
    return xf + jnp.dot(act_ref[...], wd_ref[...], preferred_element_type=_f32)


def _proj_ffn_kernel(x_ref, o_ref, wo_ref, g_ref, wgu_ref, wd_ref, out_ref, act_ref):
    x1 = x_ref[...] + jnp.dot(o_ref[...], wo_ref[...], preferred_element_type=_f32)
    out_ref[...] = _ffn_residual(x1, g_ref[...], wgu_ref, wd_ref, act_ref)


def _proj_ffn_call(x2, o2, w_o, g, w_gu, w_d):
    m = x2.shape[0]
    tok = lambda i: (i, 0)
    return pl.pallas_call(
        _proj_ffn_kernel,
        grid=(m // TOK_TILE,),
        in_specs=[
            pl.BlockSpec((TOK_TILE, D_MODEL), tok),
            pl.BlockSpec((TOK_TILE, Q_DIM), tok),
            _const_spec((Q_DIM, D_MODEL)),
            _const_spec((1, D_MODEL)),
            _const_spec((D_MODEL, 2 * D_FF)),
            _const_spec((D_FF, D_MODEL)),
        ],
        out_specs=pl.BlockSpec((TOK_TILE, D_MODEL), tok),
        out_shape=jax.ShapeDtypeStruct((m, D_MODEL), _f32),
        scratch_shapes=[pltpu.VMEM((TOK_TILE, D_FF), _bf16)],
        compiler_params=_compiler_params(1),
        name="proj_ffn",
    )(x2, o2, w_o, g, w_gu, w_d)


def _conv_ffn_kernel(x_ref, xp_ref, xn_ref, gc_ref, w1_ref, b1_ref, wdw_ref, bdw_ref, lng_ref, lnb_ref,
                     w2_ref, b2_ref, gf_ref, wgu_ref, wd_ref, gfin_ref, out_ref,
                     u_ref, c_ref, act_ref, *, tiles_per_seq):
    i = pl.program_id(0)
    first = (i % tiles_per_seq) == 0
    last = (i % tiles_per_seq) == tiles_per_seq - 1
    rows = TOK_TILE + 2 * HALO

    xf = x_ref[...]
    xh = jnp.concatenate([xp_ref[...], xf, xn_ref[...]], axis=0)
    h = _rmsnorm(xh, gc_ref[...]).astype(_bf16)
    u = jnp.dot(h, w1_ref[...], preferred_element_type=_f32) + b1_ref[...]
    u = u[:, :CONV_CH] * jax.nn.sigmoid(u[:, CONV_CH:])
    r = lax.broadcasted_iota(jnp.int32, (rows, 1), 0)
    outside = ((r < HALO) & first) | ((r >= HALO + TOK_TILE) & last)
    u = jnp.where(outside, 0.0, u)
    for t in range(N_LANE_TILES):
        u_ref[t] = u[:, t * LANES:(t + 1) * LANES]

    def conv_tile(t, carry):
        for rc in range(TOK_TILE // CONV_ROWS):
            acc = jnp.zeros((CONV_ROWS, LANES), _f32)
            for j in range(CONV_WIDTH):
                start = rc * CONV_ROWS + HALO - CONV_PAD + j
                acc = acc + u_ref[t, pl.ds(start, CONV_ROWS), :] * wdw_ref[t, pl.ds(j, 1), :]
            c_ref[t, pl.ds(rc * CONV_ROWS, CONV_ROWS), :] = acc + bdw_ref[t]
        return carry

    lax.fori_loop(0, N_LANE_TILES, conv_tile, 0)

    c = jnp.concatenate([c_ref[t] for t in range(N_LANE_TILES)], axis=1)
    mu = jnp.mean(c, axis=-1, keepdims=True)
    var = jnp.mean(jnp.square(c - mu), axis=-1, keepdims=True)
    y = (c - mu) * lax.rsqrt(var + EPS) * lng_ref[...] + lnb_ref[...]
    y = (y * jax.nn.sigmoid(y)).astype(_bf16)
    x1 = xf + jnp.dot(y, w2_ref[...], preferred_element_type=_f32) + b2_ref[...]
    x2 = _ffn_residual(x1, gf_ref[...], wgu_ref, wd_ref, act_ref)
    out_ref[...] = _rmsnorm(x2, gfin_ref[...])


def _conv_ffn_call(x2, gc, w1, b1, wdw, bdw, lng, lnb, w2, b2, gf, w_gu, w_d, gfin, seq):
    m = x2.shape[0]
    tiles_per_seq = seq // TOK_TILE
    halo_per_tile = TOK_TILE // HALO
    n_halo = m // HALO
    tok = lambda i: (i, 0)
    prev = lambda i: (jnp.maximum(i * halo_per_tile - 1, 0), 0)
    nxt = lambda i: (jnp.minimum((i + 1) * halo_per_tile, n_halo - 1), 0)
    rows = TOK_TILE + 2 * HALO
    return pl.pallas_call(
        functools.partial(_conv_ffn_kernel, tiles_per_seq=tiles_per_seq),
        grid=(m // TOK_TILE,),
        in_specs=[
            pl.BlockSpec((TOK_TILE, D_MODEL), tok),
            pl.BlockSpec((HALO, D_MODEL), prev),
            pl.BlockSpec((HALO, D_MODEL), nxt),
            _const_spec((1, D_MODEL)),
            _const_spec((D_MODEL, 2 * CONV_CH)),
            _const_spec((1, 2 * CONV_CH)),
            _const_spec((N_LANE_TILES, CONV_WIDTH, LANES)),
            _const_spec((N_LANE_TILES, 1, LANES)),
            _const_spec((1, CONV_CH)),
            _const_spec((1, CONV_CH)),
            _const_spec((CONV_CH, D_MODEL)),
            _const_spec((1, D_MODEL)),
            _const_spec((1, D_MODEL)),
            _const_spec((D_MODEL, 2 * D_FF)),
            _const_spec((D_FF, D_MODEL)),
            _const_spec((1, D_MODEL)),
        ],
        out_specs=pl.BlockSpec((TOK_TILE, D_MODEL), tok),
        out_shape=jax.ShapeDtypeStruct((m, D_MODEL), _f32),
        scratch_shapes=[
            pltpu.VMEM((N_LANE_TILES, rows, LANES), _f32),
            pltpu.VMEM((N_LANE_TILES, TOK_TILE, LANES), _f32),
            pltpu.VMEM((TOK_TILE, D_FF), _bf16),
        ],
        compiler_params=_compiler_params(1),
        name="conv_ffn",
    )(x2, x2, x2, gc, w1, b1, wdw, bdw, lng, lnb, w2, b2, gf, w_gu, w_d, gfin)


def _rope_tables(seq):
    pos = jnp.arange(seq, dtype=_f32)
    inv_freq = ROPE_THETA ** (-jnp.arange(0, ROT_DIM, 2, dtype=_f32) / ROT_DIM)
    ang = pos[:, None] * inv_freq[None, :]
    cos = jnp.cos(ang)
    sin = jnp.sin(ang)
    ones = jnp.ones((seq, HEAD_DIM - ROT_DIM), _f32)
    zeros_half = jnp.zeros((seq, ROT_HALF), _f32)
    zeros_rest = jnp.zeros((seq, HEAD_DIM - ROT_DIM), _f32)
    cos_h = jnp.concatenate([cos, cos, ones], axis=1)
    sin_hi_h = jnp.concatenate([zeros_half, sin, zeros_rest], axis=1)
    sin_lo_h = jnp.concatenate([-sin, zeros_half, zeros_rest], axis=1)
    rep = lambda a: jnp.tile(a, (1, HEADS_PER_TILE))
    return rep(cos_h), rep(sin_hi_h), rep(sin_lo_h)


def kernel(x, attn_norm, attn_w_qkv, attn_w_o, attn_sink, conv_norm, conv_w_pw1, conv_b_pw1, conv_w_dw,
           conv_b_dw, conv_ln_g, conv_ln_b, conv_w_pw2, conv_b_pw2, ffn_norm, ffn_w_gu, ffn_w_down, final_norm):
    b, s, d = x.shape
    assert d == D_MODEL and s % TOK_TILE == 0 and s % (ATT_QBLKS * BLK) == 0
    assert attn_norm.shape[0] == 1 and conv_norm.shape[0] == 1 and ffn_norm.shape[0] == 2
    m = b * s
    x2 = x.reshape(m, d)
    row = lambda v: v.reshape(1, -1).astype(_f32)
    bf = lambda w: w.astype(_bf16)

    cos_t, sin_hi_t, sin_lo_t = _rope_tables(s)
    q, k, v = _qkv_call(x2, row(attn_norm[0]), bf(attn_w_qkv[0]), cos_t, sin_hi_t, sin_lo_t, s)
    o = _attn_call(attn_sink[0].astype(_f32), q.reshape(b, s, Q_DIM), k.reshape(b, s, KV_DIM),
                   v.reshape(b, s, KV_DIM))
    x2 = _proj_ffn_call(x2, o.reshape(m, Q_DIM), bf(attn_w_o[0]), row(ffn_norm[0]),
                        bf(ffn_w_gu[0]), bf(ffn_w_down[0]))

    wdw = conv_w_dw[0].reshape(CONV_WIDTH, N_LANE_TILES, LANES).transpose(1, 0, 2)
    bdw = conv_b_dw[0].reshape(N_LANE_TILES, 1, LANES)
    out = _conv_ffn_call(x2, row(conv_norm[0]), bf(conv_w_pw1[0]), row(conv_b_pw1[0]), wdw, bdw,
                         row(conv_ln_g[0]), row(conv_ln_b[0]), bf(conv_w_pw2[0]), row(conv_b_pw2[0]),
                         row(ffn_norm[1]), bf(ffn_w_gu[1]), bf(ffn_w_down[1]), row(final_norm), s)
    return out.reshape(b, s, d)
```

```python
import functools
import math

import jax
import jax.numpy as jnp
from jax import lax
from jax.experimental import pallas as pl
from jax.experimental.pallas import tpu as pltpu

D_MODEL = 1024
N_HEADS = 16
N_KV_HEADS = 4
HEAD_DIM = D_MODEL // N_HEADS
GROUP = N_HEADS // N_KV_HEADS
ROT_DIM = HEAD_DIM // 4
ROT_HALF = ROT_DIM // 2
ROPE_THETA = 500000.0
WINDOW = 128
BLK = 128
Q_DIM = N_HEADS * HEAD_DIM
KV_DIM = N_KV_HEADS * HEAD_DIM
QKV_DIM = Q_DIM + 2 * KV_DIM
CONV_CH = D_MODEL
CONV_WIDTH = 31
CONV_PAD = (CONV_WIDTH - 1) // 2
D_FF = ((8 * D_MODEL // 3 + 255) // 256) * 256
EPS = 1e-6
NEG = -1e30

LANES = 128
SUBLANES = 8
VMEM_LIMIT_BYTES = 56 * 1024 * 1024

TOK_TILE = 512
FF_CHUNK = 256
HALO = 2 * SUBLANES
CONV_ROWS = 128
N_LANE_TILES = D_MODEL // LANES
HEADS_PER_TILE = LANES // HEAD_DIM
PAIRS_PER_GROUP = GROUP // HEADS_PER_TILE
ATT_QBLKS = 4
ATT_KBLKS = ATT_QBLKS + 2
PAIR_ROWS = HEADS_PER_TILE * BLK

_f32 = jnp.float32
_bf16 = jnp.bfloat16


def _rmsnorm(xf, g):
    y = xf * lax.rsqrt(jnp.mean(xf * xf, axis=-1, keepdims=True) + EPS)
    return y * g


def _const_spec(shape):
    nd = len(shape)
    return pl.BlockSpec(shape, lambda *_: (0,) * nd, pipeline_mode=pl.Buffered(1))


def _compiler_params(n_axes):
    return pltpu.CompilerParams(
        dimension_semantics=("arbitrary",) * n_axes,
        vmem_limit_bytes=VMEM_LIMIT_BYTES,
    )


def _qkv_kernel(x_ref, g_ref, w_ref, cos_ref, sin_hi_ref, sin_lo_ref, q_ref, k_ref, v_ref):
    h = _rmsnorm(x_ref[...], g_ref[...]).astype(_bf16)
    qkv = jnp.dot(h, w_ref[...], preferred_element_type=_f32)
    cos = cos_ref[...]
    sin_hi = sin_hi_ref[...]
    sin_lo = sin_lo_ref[...]
    scale = 1.0 / math.sqrt(HEAD_DIM)

    def rope(t):
        return (t * cos + pltpu.roll(t, ROT_HALF, 1) * sin_hi
                + pltpu.roll(t, LANES - ROT_HALF, 1) * sin_lo)

    for j in range(Q_DIM // LANES):
        t = qkv[:, j * LANES:(j + 1) * LANES]
        q_ref[:, j * LANES:(j + 1) * LANES] = (rope(t) * scale).astype(_bf16)
    for j in range(KV_DIM // LANES):
        t = qkv[:, Q_DIM + j * LANES:Q_DIM + (j + 1) * LANES]
        k_ref[:, j * LANES:(j + 1) * LANES] = rope(t).astype(_bf16)
    v_ref[...] = qkv[:, Q_DIM + KV_DIM:].astype(_bf16)


def _qkv_call(x2, g, w_qkv, cos_t, sin_hi_t, sin_lo_t, seq):
    m = x2.shape[0]
    tiles_per_seq = seq // TOK_TILE
    tok = lambda i: (i, 0)
    pos = lambda i: (i % tiles_per_seq, 0)
    return pl.pallas_call(
        _qkv_kernel,
        grid=(m // TOK_TILE,),
        in_specs=[
            pl.BlockSpec((TOK_TILE, D_MODEL), tok),
            _const_spec((1, D_MODEL)),
            _const_spec((D_MODEL, QKV_DIM)),
            pl.BlockSpec((TOK_TILE, LANES), pos),
            pl.BlockSpec((TOK_TILE, LANES), pos),
            pl.BlockSpec((TOK_TILE, LANES), pos),
        ],
        out_specs=[
            pl.BlockSpec((TOK_TILE, Q_DIM), tok),
            pl.BlockSpec((TOK_TILE, KV_DIM), tok),
            pl.BlockSpec((TOK_TILE, KV_DIM), tok),
        ],
        out_shape=[
            jax.ShapeDtypeStruct((m, Q_DIM), _bf16),
            jax.ShapeDtypeStruct((m, KV_DIM), _bf16),
            jax.ShapeDtypeStruct((m, KV_DIM), _bf16),
        ],
        compiler_params=_compiler_params(1),
        name="qkv_rope",
    )(x2, g, w_qkv, cos_t, sin_hi_t, sin_lo_t)


def _attn_kernel(sink_ref, q_ref, kp_ref, kc_ref, kn_ref, vp_ref, vc_ref, vn_ref, o_ref, kz_ref, vz_ref):
    step = pl.program_id(1)
    n_steps = pl.num_programs(1)

    lane = lax.broadcasted_iota(jnp.int32, (BLK, LANES), 1)
    low_lanes = lane < HEAD_DIM
    low_lanes_f32 = (lane & HEAD_DIM) == 0
    zero = jnp.zeros((), _bf16)
    high_flag = (lane // HEAD_DIM).astype(_f32)
    ones_low = (1.0 - high_flag).astype(_bf16)
    ones_high = high_flag.astype(_bf16)

    for kb in range(ATT_KBLKS):
        if kb == 0:
            k_blk, v_blk = kp_ref[...], vp_ref[...]
        elif kb == ATT_KBLKS - 1:
            k_blk, v_blk = kn_ref[...], vn_ref[...]
        else:
            rows = slice((kb - 1) * BLK, kb * BLK)
            k_blk, v_blk = kc_ref[rows, :], vc_ref[rows, :]
        for g in range(N_KV_HEADS):
            tile = slice((g // HEADS_PER_TILE) * LANES, (g // HEADS_PER_TILE + 1) * LANES)
            own = low_lanes if g % HEADS_PER_TILE == 0 else ~low_lanes
            k_own = jnp.where(own, k_blk[:, tile], zero)
            v_own = jnp.where(own, v_blk[:, tile], zero)
            k_swap = pltpu.roll(k_own, HEAD_DIM, 1)
            v_swap = pltpu.roll(v_own, HEAD_DIM, 1)
            if g % HEADS_PER_TILE == 0:
                k_lo, k_hi, v_lo, v_hi = k_own, k_swap, v_own, v_swap
            else:
                k_lo, k_hi, v_lo, v_hi = k_swap, k_own, v_swap, v_own
            base = kb * PAIR_ROWS
            kz_ref[g, base:base + BLK, :] = k_lo
            kz_ref[g, base + BLK:base + PAIR_ROWS, :] = k_hi
            vz_ref[g, base:base + BLK, :LANES] = v_lo
            vz_ref[g, base + BLK:base + PAIR_ROWS, :LANES] = v_hi
            vz_ref[g, base:base + BLK, LANES:] = ones_low
            vz_ref[g, base + BLK:base + PAIR_ROWS, LANES:] = ones_high

    qi = lax.broadcasted_iota(jnp.int32, (BLK, BLK), 0)
    ci = lax.broadcasted_iota(jnp.int32, (BLK, BLK), 1)
    prev_in_window = ci >= qi
    next_in_window = ci <= qi

    def q_block(j, carry):
        rows = pl.ds(pl.multiple_of(j * BLK, BLK), BLK)
        stack = pl.ds(pl.multiple_of(j * PAIR_ROWS, PAIR_ROWS), 3 * PAIR_ROWS)
        prev_ok = prev_in_window & ((step > 0) | (j > 0))
        next_ok = next_in_window & ((step < n_steps - 1) | (j < ATT_QBLKS - 1))
        for g in range(N_KV_HEADS):
            for pair in range(PAIRS_PER_GROUP):
                qt = g * PAIRS_PER_GROUP + pair
                q2 = q_ref[rows, qt * LANES:(qt + 1) * LANES]
                s = lax.dot_general(q2, kz_ref[g, stack, :], (((1,), (1,)), ((), ())),
                                    preferred_element_type=_f32)
                probs = [None] * (3 * HEADS_PER_TILE)
                sink_terms = []
                for half in range(HEADS_PER_TILE):
                    sink = sink_ref[qt * HEADS_PER_TILE + half]
                    col = lambda kb: slice((kb * HEADS_PER_TILE + half) * BLK, (kb * HEADS_PER_TILE + half + 1) * BLK)
                    s_prev = jnp.where(prev_ok, s[:, col(0)], NEG)
                    s_own = s[:, col(1)]
                    s_next = jnp.where(next_ok, s[:, col(2)], NEG)
                    mx = jnp.max(jnp.maximum(jnp.maximum(s_prev, s_own), s_next), axis=-1, keepdims=True)
                    mx = jnp.maximum(mx, sink)
                    probs[0 * HEADS_PER_TILE + half] = jnp.exp(s_prev - mx).astype(_bf16)
                    probs[1 * HEADS_PER_TILE + half] = jnp.exp(s_own - mx).astype(_bf16)
                    probs[2 * HEADS_PER_TILE + half] = jnp.exp(s_next - mx).astype(_bf16)
                    sink_terms.append(jnp.exp(sink - mx))
                p = jnp.concatenate(probs, axis=1)
                o_ext = jnp.dot(p, vz_ref[g, stack, :], preferred_element_type=_f32)
                denom = o_ext[:, LANES:] + jnp.where(low_lanes_f32, sink_terms[0], sink_terms[1])
                o_ref[rows, qt * LANES:(qt + 1) * LANES] = (o_ext[:, :LANES] / denom).astype(_bf16)
        return carry

    lax.fori_loop(0, ATT_QBLKS, q_block, 0)


def _attn_call(sink, q3, k3, v3):
    b, s, _ = q3.shape
    nb = s // BLK
    rows = ATT_QBLKS * BLK
    own = lambda bi, n: (bi, n, 0)
    prev = lambda bi, n: (bi, jnp.maximum(n * ATT_QBLKS - 1, 0), 0)
    nxt = lambda bi, n: (bi, jnp.minimum((n + 1) * ATT_QBLKS, nb - 1), 0)
    edge_spec = lambda imap: pl.BlockSpec((None, BLK, KV_DIM), imap)
    own_spec = pl.BlockSpec((None, rows, KV_DIM), own)
    return pl.pallas_call(
        _attn_kernel,
        grid=(b, s // rows),
        in_specs=[
            pl.BlockSpec(memory_space=pltpu.SMEM),
            pl.BlockSpec((None, rows, Q_DIM), own),
            edge_spec(prev), own_spec, edge_spec(nxt),
            edge_spec(prev), own_spec, edge_spec(nxt),
        ],
        out_specs=pl.BlockSpec((None, rows, Q_DIM), own),
        out_shape=jax.ShapeDtypeStruct((b, s, Q_DIM), _bf16),
        scratch_shapes=[
            pltpu.VMEM((N_KV_HEADS, ATT_KBLKS * PAIR_ROWS, LANES), _bf16),
            pltpu.VMEM((N_KV_HEADS, ATT_KBLKS * PAIR_ROWS, 2 * LANES), _bf16),
        ],
        compiler_params=_compiler_params(2),
        name="window_attn",
    )(sink, q3, k3, k3, k3, v3, v3, v3)


def _spread(thunks, n_slots):
    bounds = [(len(thunks) * i) // n_slots for i in range(n_slots + 1)]
    return [thunks[bounds[i]:bounds[i + 1]] for i in range(n_slots)]


def _ffn_residual(xf, g, wgu_ref, wd_ref, act_ref, up_fillers=(), down_fillers=()):
    n_up = D_FF // FF_CHUNK
    n_down = D_MODEL // FF_CHUNK if down_fillers else 1
    down_cols = D_MODEL // n_down
    up_groups = _spread(list(up_fillers), n_up)
    down_groups = _spread(list(down_fillers), n_down)
    h = _rmsnorm(xf, g).astype(_bf16)
    for c in range(n_up):
        lo = c * FF_CHUNK
        gate = jnp.dot(h, wgu_ref[:, lo:lo + FF_CHUNK], preferred_element_type=_f32)
        up = jnp.dot(h, wgu_ref[:, D_FF + lo:D_FF + lo + FF_CHUNK], preferred_element_type=_f32)
        act = gate * jax.nn.sigmoid(gate) * up
        act_ref[:, lo:lo + FF_CHUNK] = act.astype(_bf16)
        if up_groups[c]:
            zeros_after = jnp.minimum(jnp.abs(act[:, :LANES]), 0.0)
            n_fill = len(up_groups[c])
            for i, thunk in enumerate(up_groups[c]):
                thunk(zeros_after, (i * act.shape[0]) // n_fill, ((i + 1) * act.shape[0]) // n_fill)
    outs = []
    for c in range(n_down):
        lo = c * down_cols
        outs.append(xf[:, lo:lo + down_cols]
                    + jnp.dot(act_ref[...], wd_ref[:, lo:lo + down_cols], preferred_element_type=_f32))
        for thunk in down_groups[c]:
            thunk()
    return outs[0] if n_down == 1 else jnp.concatenate(outs, axis=1)


def _proj_ffn_kernel(x_ref, o_ref, wo_ref, g_ref, wgu_ref, wd_ref, out_ref, act_ref):
    x1 = x_ref[...] + jnp.dot(o_ref[...], wo_ref[...], preferred_element_type=_f32)
    out_ref[...] = _ffn_residual(x1, g_ref[...], wgu_ref, wd_ref, act_ref)


def _proj_ffn_call(x2, o2, w_o, g, w_gu, w_d):
    m = x2.shape[0]
    tok = lambda i: (i, 0)
    return pl.pallas_call(
        _proj_ffn_kernel,
        grid=(m // TOK_TILE,),
        in_specs=[
            pl.BlockSpec((TOK_TILE, D_MODEL), tok),
            pl.BlockSpec((TOK_TILE, Q_DIM), tok),
            _const_spec((Q_DIM, D_MODEL)),
            _const_spec((1, D_MODEL)),
            _const_spec((D_MODEL, 2 * D_FF)),
            _const_spec((D_FF, D_MODEL)),
        ],
        out_specs=pl.BlockSpec((TOK_TILE, D_MODEL), tok),
        out_shape=jax.ShapeDtypeStruct((m, D_MODEL), _f32),
        scratch_shapes=[pltpu.VMEM((TOK_TILE, D_FF), _bf16)],
        compiler_params=_compiler_params(1),
        name="proj_ffn",
    )(x2, o2, w_o, g, w_gu, w_d)


def _conv_ffn_kernel(xa_ref, xp_ref, xn_ref, xb_ref, gc_ref, w1_ref, b1_ref, wdw_ref, bdw_ref, lng_ref, lnb_ref,
                     w2_ref, b2_ref, gf_ref, wgu_ref, wd_ref, gfin_ref, out_ref,
                     u_ref, c_ref, y_ref, act_ref, *, n_tiles, tiles_per_seq):
    s = pl.program_id(0)
    rows = TOK_TILE + 2 * HALO

    @pl.when(s == 0)
    def _():
        y_ref[...] = jnp.zeros_like(y_ref)

    x1 = xb_ref[...] + jnp.dot(y_ref[...], w2_ref[...], preferred_element_type=_f32) + b2_ref[...]

    t_idx = jnp.minimum(s, n_tiles - 1)
    first = (t_idx % tiles_per_seq) == 0
    last = (t_idx % tiles_per_seq) == tiles_per_seq - 1
    xh = jnp.concatenate([xp_ref[...], xa_ref[...], xn_ref[...]], axis=0)
    h = _rmsnorm(xh, gc_ref[...]).astype(_bf16)
    u = jnp.dot(h, w1_ref[...], preferred_element_type=_f32) + b1_ref[...]
    u = u[:, :CONV_CH] * jax.nn.sigmoid(u[:, CONV_CH:])
    r = lax.broadcasted_iota(jnp.int32, (rows, 1), 0)
    outside = ((r < HALO) & first) | ((r >= HALO + TOK_TILE) & last)
    u = jnp.where(outside, 0.0, u)
    for t in range(N_LANE_TILES):
        u_ref[t] = u[:, t * LANES:(t + 1) * LANES]

    def conv_unit(t, rc):
        def run(zeros, row_lo, row_hi):
            acc = jnp.zeros((CONV_ROWS, LANES), _f32)
            for j in range(CONV_WIDTH):
                start = rc * CONV_ROWS + HALO - CONV_PAD + j
                gate_row = (row_lo + (j * (row_hi - row_lo)) // CONV_WIDTH) // SUBLANES * SUBLANES
                w_row = wdw_ref[t, pl.ds(j, 1), :] + zeros[gate_row:gate_row + 1, :]
                acc = acc + u_ref[t, pl.ds(start, CONV_ROWS), :] * w_row
            c_ref[t, pl.ds(rc * CONV_ROWS, CONV_ROWS), :] = acc + bdw_ref[t]
        return run

    def norm_unit(rc):
        def run():
            rsl = pl.ds(rc * CONV_ROWS, CONV_ROWS)
            c = jnp.concatenate([c_ref[t, rsl, :] for t in range(N_LANE_TILES)], axis=1)
            mu = jnp.mean(c, axis=-1, keepdims=True)
            var = jnp.mean(jnp.square(c - mu), axis=-1, keepdims=True)
            y = (c - mu) * lax.rsqrt(var + EPS) * lng_ref[...] + lnb_ref[...]
            y_ref[rsl, :] = (y * jax.nn.sigmoid(y)).astype(_bf16)
        return run

    n_rc = TOK_TILE // CONV_ROWS
    conv_units = [conv_unit(t, rc) for t in range(N_LANE_TILES) for rc in range(n_rc)]
    norm_units = [norm_unit(rc) for rc in range(n_rc)]

    x2 = _ffn_residual(x1, gf_ref[...], wgu_ref, wd_ref, act_ref, conv_units, norm_units)
    out_ref[...] = _rmsnorm(x2, gfin_ref[...])


def _conv_ffn_call(x2, gc, w1, b1, wdw, bdw, lng, lnb, w2, b2, gf, w_gu, w_d, gfin, seq):
    m = x2.shape[0]
    n_tiles = m // TOK_TILE
    tiles_per_seq = seq // TOK_TILE
    halo_per_tile = TOK_TILE // HALO
    n_halo = m // HALO
    front = lambda s: jnp.minimum(s, n_tiles - 1)
    back = lambda s: jnp.maximum(s - 1, 0)
    tok_front = lambda s: (front(s), 0)
    tok_back = lambda s: (back(s), 0)
    prev = lambda s: (jnp.maximum(front(s) * halo_per_tile - 1, 0), 0)
    nxt = lambda s: (jnp.minimum((front(s) + 1) * halo_per_tile, n_halo - 1), 0)
    rows = TOK_TILE + 2 * HALO
    return pl.pallas_call(
        functools.partial(_conv_ffn_kernel, n_tiles=n_tiles, tiles_per_seq=tiles_per_seq),
        grid=(n_tiles + 1,),
        in_specs=[
            pl.BlockSpec((TOK_TILE, D_MODEL), tok_front),
            pl.BlockSpec((HALO, D_MODEL), prev),
            pl.BlockSpec((HALO, D_MODEL), nxt),
            pl.BlockSpec((TOK_TILE, D_MODEL), tok_back),
            _const_spec((1, D_MODEL)),
            _const_spec((D_MODEL, 2 * CONV_CH)),
            _const_spec((1, 2 * CONV_CH)),
            _const_spec((N_LANE_TILES, CONV_WIDTH, LANES)),
            _const_spec((N_LANE_TILES, 1, LANES)),
            _const_spec((1, CONV_CH)),
            _const_spec((1, CONV_CH)),
            _const_spec((CONV_CH, D_MODEL)),
            _const_spec((1, D_MODEL)),
            _const_spec((1, D_MODEL)),
            _const_spec((D_MODEL, 2 * D_FF)),
            _const_spec((D_FF, D_MODEL)),
            _const_spec((1, D_MODEL)),
        ],
        out_specs=pl.BlockSpec((TOK_TILE, D_MODEL), tok_back),
        out_shape=jax.ShapeDtypeStruct((m, D_MODEL), _f32),
        scratch_shapes=[
            pltpu.VMEM((N_LANE_TILES, rows, LANES), _f32),
            pltpu.VMEM((N_LANE_TILES, TOK_TILE, LANES), _f32),
            pltpu.VMEM((TOK_TILE, CONV_CH), _bf16),
            pltpu.VMEM((TOK_TILE, D_FF), _bf16),
        ],
        compiler_params=_compiler_params(1),
        name="conv_ffn",
    )(x2, x2, x2, x2, gc, w1, b1, wdw, bdw, lng, lnb, w2, b2, gf, w_gu, w_d, gfin)


def _rope_tables(seq):
    pos = jnp.arange(seq, dtype=_f32)
    inv_freq = ROPE_THETA ** (-jnp.arange(0, ROT_DIM, 2, dtype=_f32) / ROT_DIM)
    ang = pos[:, None] * inv_freq[None, :]
    cos = jnp.cos(ang)
    sin = jnp.sin(ang)
    ones = jnp.ones((seq, HEAD_DIM - ROT_DIM), _f32)
    zeros_half = jnp.zeros((seq, ROT_HALF), _f32)
    zeros_rest = jnp.zeros((seq, HEAD_DIM - ROT_DIM), _f32)
    cos_h = jnp.concatenate([cos, cos, ones], axis=1)
    sin_hi_h = jnp.concatenate([zeros_half, sin, zeros_rest], axis=1)
    sin_lo_h = jnp.concatenate([-sin, zeros_half, zeros_rest], axis=1)
    rep = lambda a: jnp.tile(a, (1, HEADS_PER_TILE))
    return rep(cos_h), rep(sin_hi_h), rep(sin_lo_h)


def kernel(x, attn_norm, attn_w_qkv, attn_w_o, attn_sink, conv_norm, conv_w_pw1, conv_b_pw1, conv_w_dw,
           conv_b_dw, conv_ln_g, conv_ln_b, conv_w_pw2, conv_b_pw2, ffn_norm, ffn_w_gu, ffn_w_down, final_norm):
    b, s, d = x.shape
    assert d == D_MODEL and s % TOK_TILE == 0 and s % (ATT_QBLKS * BLK) == 0
    assert attn_norm.shape[0] == 1 and conv_norm.shape[0] == 1 and ffn_norm.shape[0] == 2
    m = b * s
    x2 = x.reshape(m, d)
    row = lambda v: v.reshape(1, -1).astype(_f32)
    bf = lambda w: w.astype(_bf16)

    cos_t, sin_hi_t, sin_lo_t = _rope_tables(s)
    q, k, v = _qkv_call(x2, row(attn_norm[0]), bf(attn_w_qkv[0]), cos_t, sin_hi_t, sin_lo_t, s)
    o = _attn_call(attn_sink[0].astype(_f32), q.reshape(b, s, Q_DIM), k.reshape(b, s, KV_DIM),
                   v.reshape(b, s, KV_DIM))
    x2 = _proj_ffn_call(x2, o.reshape(m, Q_DIM), bf(attn_w_o[0]), row(ffn_norm[0]),
                        bf(ffn_w_gu[0]), bf(ffn_w_down[0]))

    wdw = conv_w_dw[0].reshape(CONV_WIDTH, N_LANE_TILES, LANES).transpose(1, 0, 2)
    bdw = conv_b_dw[0].reshape(N_LANE_TILES, 1, LANES)
    out = _conv_ffn_call(x2, row(conv_norm[0]), bf(conv_w_pw1[0]), row(conv_b_pw1[0]), wdw, bdw,
                         row(conv_ln_g[0]), row(conv_ln_b[0]), bf(conv_w_pw2[0]), row(conv_b_pw2[0]),
                         row(ffn_norm[1]), bf(ffn_w_gu[1]), bf(ffn_w_down[1]), row(final_norm), s)
    return out.reshape(b, s, d)
```

```python
import functools
import math

import jax
import jax.numpy as jnp
from jax import lax
from jax.experimental import pallas as pl
from jax.experimental.pallas import tpu as pltpu

D_MODEL = 1024
N_HEADS = 16
N_KV_HEADS = 4
HEAD_DIM = D_MODEL // N_HEADS
GROUP = N_HEADS // N_KV_HEADS
ROT_DIM = HEAD_DIM // 4
ROT_HALF = ROT_DIM // 2
ROPE_THETA = 500000.0
WINDOW = 128
BLK = 128
Q_DIM = N_HEADS * HEAD_DIM
KV_DIM = N_KV_HEADS * HEAD_DIM
QKV_DIM = Q_DIM + 2 * KV_DIM
CONV_CH = D_MODEL
CONV_WIDTH = 31
CONV_PAD = (CONV_WIDTH - 1) // 2
D_FF = ((8 * D_MODEL // 3 + 255) // 256) * 256
EPS = 1e-6
NEG = -1e30

LANES = 128
SUBLANES = 8
VMEM_LIMIT_BYTES = 56 * 1024 * 1024

TOK_TILE = 512
FF_CHUNK = 256
HALO = 2 * SUBLANES
CONV_ROWS = 128
N_LANE_TILES = D_MODEL // LANES
HEADS_PER_TILE = LANES // HEAD_DIM
PAIRS_PER_GROUP = GROUP // HEADS_PER_TILE
ATT_QBLKS = TOK_TILE // BLK
ATT_KBLKS = ATT_QBLKS + 2
ATT_UNITS = ATT_QBLKS * (Q_DIM // LANES)
ATT_VALUE_LAG = 3
PAIR_ROWS = HEADS_PER_TILE * BLK

_f32 = jnp.float32
_bf16 = jnp.bfloat16


def _rmsnorm(xf, g):
    y = xf * lax.rsqrt(jnp.mean(xf * xf, axis=-1, keepdims=True) + EPS)
    return y * g


def _const_spec(shape, index=None):
    index = (0,) * len(shape) if index is None else index
    return pl.BlockSpec(shape, lambda *_: index, pipeline_mode=pl.Buffered(1))


def _compiler_params(n_axes):
    return pltpu.CompilerParams(
        dimension_semantics=("arbitrary",) * n_axes,
        vmem_limit_bytes=VMEM_LIMIT_BYTES,
    )


def _qkv_kernel(x_ref, g_ref, w_ref, cos_ref, sin_hi_ref, sin_lo_ref, q_ref, k_ref, v_ref):
    h = _rmsnorm(x_ref[...], g_ref[...]).astype(_bf16)
    qkv = jnp.dot(h, w_ref[...], preferred_element_type=_f32)
    cos = cos_ref[...]
    sin_hi = sin_hi_ref[...]
    sin_lo = sin_lo_ref[...]
    scale = 1.0 / math.sqrt(HEAD_DIM)

    def rope(t):
        return (t * cos + pltpu.roll(t, ROT_HALF, 1) * sin_hi
                + pltpu.roll(t, LANES - ROT_HALF, 1) * sin_lo)

    for j in range(Q_DIM // LANES):
        t = qkv[:, j * LANES:(j + 1) * LANES]
        q_ref[:, j * LANES:(j + 1) * LANES] = (rope(t) * scale).astype(_bf16)
    for j in range(KV_DIM // LANES):
        t = qkv[:, Q_DIM + j * LANES:Q_DIM + (j + 1) * LANES]
        k_ref[:, j * LANES:(j + 1) * LANES] = rope(t).astype(_bf16)
    v_ref[...] = qkv[:, Q_DIM + KV_DIM:].astype(_bf16)


def _qkv_call(x2, g, w_qkv, cos_t, sin_hi_t, sin_lo_t, seq):
    m = x2.shape[0]
    tiles_per_seq = seq // TOK_TILE
    tok = lambda i: (i, 0)
    pos = lambda i: (i % tiles_per_seq, 0)
    return pl.pallas_call(
        _qkv_kernel,
        grid=(m // TOK_TILE,),
        in_specs=[
            pl.BlockSpec((TOK_TILE, D_MODEL), tok),
            _const_spec((1, D_MODEL)),
            _const_spec((D_MODEL, QKV_DIM)),
            pl.BlockSpec((TOK_TILE, LANES), pos),
            pl.BlockSpec((TOK_TILE, LANES), pos),
            pl.BlockSpec((TOK_TILE, LANES), pos),
        ],
        out_specs=[
            pl.BlockSpec((TOK_TILE, Q_DIM), tok),
            pl.BlockSpec((TOK_TILE, KV_DIM), tok),
            pl.BlockSpec((TOK_TILE, KV_DIM), tok),
        ],
        out_shape=[
            jax.ShapeDtypeStruct((m, Q_DIM), _bf16),
            jax.ShapeDtypeStruct((m, KV_DIM), _bf16),
            jax.ShapeDtypeStruct((m, KV_DIM), _bf16),
        ],
        compiler_params=_compiler_params(1),
        name="qkv_rope",
    )(x2, g, w_qkv, cos_t, sin_hi_t, sin_lo_t)


def _spread(thunks, n_slots):
    bounds = [(len(thunks) * i) // n_slots for i in range(n_slots + 1)]
    return [thunks[bounds[i]:bounds[i + 1]] for i in range(n_slots)]


def _run_fillers(group, produced):
    if not group:
        return
    zeros = jnp.minimum(jnp.abs(produced[:, :LANES]), 0.0)
    n_rows = produced.shape[0]
    for i, thunk in enumerate(group):
        thunk(zeros, (i * n_rows) // len(group), ((i + 1) * n_rows) // len(group))


def _ffn_residual(xf, g, wgu_ref, wd_ref, act_ref, fillers, fillers_have_matmuls):
    n_up = D_FF // FF_CHUNK
    n_down = D_MODEL // FF_CHUNK
    groups = _spread(list(fillers), n_up + n_down)
    h = _rmsnorm(xf, g).astype(_bf16)
    previous = xf
    for c in range(n_up):
        lo = c * FF_CHUNK
        gate = jnp.dot(h, wgu_ref[:, lo:lo + FF_CHUNK], preferred_element_type=_f32)
        up = jnp.dot(h, wgu_ref[:, D_FF + lo:D_FF + lo + FF_CHUNK], preferred_element_type=_f32)
        act = gate * jax.nn.sigmoid(gate) * up
        act_ref[:, lo:lo + FF_CHUNK] = act.astype(_bf16)
        _run_fillers(groups[c], previous if fillers_have_matmuls else act)
        previous = act
    outs = []
    for c in range(n_down):
        lo = c * FF_CHUNK
        out = xf[:, lo:lo + FF_CHUNK] + jnp.dot(act_ref[...], wd_ref[:, lo:lo + FF_CHUNK],
                                                 preferred_element_type=_f32)
        outs.append(out)
        _run_fillers(groups[n_up + c], previous if fillers_have_matmuls else out)
        previous = out
    return jnp.concatenate(outs, axis=1)


def _stage_kv_stacks(kp_ref, kc_ref, kn_ref, vp_ref, vc_ref, vn_ref, kz_ref, vz_ref):
    lane = lax.broadcasted_iota(jnp.int32, (BLK, LANES), 1)
    low_lanes = lane < HEAD_DIM
    zero = jnp.zeros((), _bf16)
    high_flag = (lane // HEAD_DIM).astype(_f32)
    ones_low = (1.0 - high_flag).astype(_bf16)
    ones_high = high_flag.astype(_bf16)
    for kb in range(ATT_KBLKS):
        if kb == 0:
            k_blk, v_blk = kp_ref[...], vp_ref[...]
        elif kb == ATT_KBLKS - 1:
            k_blk, v_blk = kn_ref[...], vn_ref[...]
        else:
            rows = slice((kb - 1) * BLK, kb * BLK)
            k_blk, v_blk = kc_ref[rows, :], vc_ref[rows, :]
        for g in range(N_KV_HEADS):
            tile = slice((g // HEADS_PER_TILE) * LANES, (g // HEADS_PER_TILE + 1) * LANES)
            own = low_lanes if g % HEADS_PER_TILE == 0 else ~low_lanes
            k_own = jnp.where(own, k_blk[:, tile], zero)
            v_own = jnp.where(own, v_blk[:, tile], zero)
            k_swap = pltpu.roll(k_own, HEAD_DIM, 1)
            v_swap = pltpu.roll(v_own, HEAD_DIM, 1)
            if g % HEADS_PER_TILE == 0:
                k_lo, k_hi, v_lo, v_hi = k_own, k_swap, v_own, v_swap
            else:
                k_lo, k_hi, v_lo, v_hi = k_swap, k_own, v_swap, v_own
            base = kb * PAIR_ROWS
            kz_ref[g, base:base + BLK, :] = k_lo
            kz_ref[g, base + BLK:base + PAIR_ROWS, :] = k_hi
            vz_ref[g, base:base + BLK, :LANES] = v_lo
            vz_ref[g, base + BLK:base + PAIR_ROWS, :LANES] = v_hi
            vz_ref[g, base:base + BLK, LANES:] = ones_low
            vz_ref[g, base + BLK:base + PAIR_ROWS, LANES:] = ones_high


def _attn_pair_unit(j, qt, prev_ok, next_ok, low_lanes_f32, sink_ref, q_ref, kz_ref, vz_ref, o_ref,
                    p_ref, sink_term_ref):
    g = qt // PAIRS_PER_GROUP
    unit = j * (Q_DIM // LANES) + qt
    rows = slice(j * BLK, (j + 1) * BLK)
    stack = slice(j * PAIR_ROWS, (j + 3) * PAIR_ROWS)

    def scores(zeros, row_lo, row_hi):
        del row_hi
        gate_row = row_lo // SUBLANES * SUBLANES
        q2 = q_ref[rows, qt * LANES:(qt + 1) * LANES] + zeros[gate_row:gate_row + 1, :].astype(_bf16)
        s = lax.dot_general(q2, kz_ref[g, stack, :], (((1,), (1,)), ((), ())),
                            preferred_element_type=_f32)
        probs = [None] * (3 * HEADS_PER_TILE)
        sink_terms = []
        for half in range(HEADS_PER_TILE):
            sink = sink_ref[qt * HEADS_PER_TILE + half]
            col = lambda kb: slice((kb * HEADS_PER_TILE + half) * BLK, (kb * HEADS_PER_TILE + half + 1) * BLK)
            s_prev = jnp.where(prev_ok, s[:, col(0)], NEG)
            s_own = s[:, col(1)]
            s_next = jnp.where(next_ok, s[:, col(2)], NEG)
            mx = jnp.max(jnp.maximum(jnp.maximum(s_prev, s_own), s_next), axis=-1, keepdims=True)
            mx = jnp.maximum(mx, sink)
            probs[0 * HEADS_PER_TILE + half] = jnp.exp(s_prev - mx).astype(_bf16)
            probs[1 * HEADS_PER_TILE + half] = jnp.exp(s_own - mx).astype(_bf16)
            probs[2 * HEADS_PER_TILE + half] = jnp.exp(s_next - mx).astype(_bf16)
            sink_terms.append(jnp.exp(sink - mx))
        p_ref[unit] = jnp.concatenate(probs, axis=1)
        sink_term_ref[unit] = jnp.where(low_lanes_f32, sink_terms[0], sink_terms[1])

    def values(zeros, row_lo, row_hi):
        del zeros, row_lo, row_hi
        o_ext = jnp.dot(p_ref[unit], vz_ref[g, stack, :], preferred_element_type=_f32)
        denom = o_ext[:, LANES:] + sink_term_ref[unit]
        o_ref[rows, qt * LANES:(qt + 1) * LANES] = (o_ext[:, :LANES] / denom).astype(_bf16)

    return scores, values


def _attn_ffn_kernel(sink_ref, q_ref, kp_ref, kc_ref, kn_ref, vp_ref, vc_ref, vn_ref, xb_ref,
                     wo_ref, g_ref, wgu_ref, wd_ref, out_ref,
                     kz_ref, vz_ref, o_ref, p_ref, sink_term_ref, act_ref, *, n_tiles, tiles_per_seq):
    s = pl.program_id(0)

    @pl.when(s == 0)
    def _():
        o_ref[...] = jnp.zeros_like(o_ref)

    x1 = xb_ref[...] + jnp.dot(o_ref[...], wo_ref[...], preferred_element_type=_f32)

    _stage_kv_stacks(kp_ref, kc_ref, kn_ref, vp_ref, vc_ref, vn_ref, kz_ref, vz_ref)
    t_idx = jnp.minimum(s, n_tiles - 1)
    has_prev = (t_idx % tiles_per_seq) != 0
    has_next = (t_idx % tiles_per_seq) != tiles_per_seq - 1
    qi = lax.broadcasted_iota(jnp.int32, (BLK, BLK), 0)
    ci = lax.broadcasted_iota(jnp.int32, (BLK, BLK), 1)
    prev_in_window = ci >= qi
    next_in_window = ci <= qi
    low_lanes_f32 = (lax.broadcasted_iota(jnp.int32, (BLK, LANES), 1) & HEAD_DIM) == 0
    halves = []
    for j in range(ATT_QBLKS):
        prev_ok = prev_in_window & has_prev if j == 0 else prev_in_window
        next_ok = next_in_window & has_next if j == ATT_QBLKS - 1 else next_in_window
        for qt in range(Q_DIM // LANES):
            halves.append(_attn_pair_unit(j, qt, prev_ok, next_ok, low_lanes_f32, sink_ref, q_ref,
                                          kz_ref, vz_ref, o_ref, p_ref, sink_term_ref))
    units = []
    for i in range(len(halves) + ATT_VALUE_LAG):
        if i < len(halves):
            units.append(halves[i][0])
        if i >= ATT_VALUE_LAG:
            units.append(halves[i - ATT_VALUE_LAG][1])

    out_ref[...] = _ffn_residual(x1, g_ref[...], wgu_ref, wd_ref, act_ref, units, fillers_have_matmuls=True)


def _attn_ffn_call(x2, q, k, v, sink, w_o, g, w_gu, w_d, seq):
    m = x2.shape[0]
    n_tiles = m // TOK_TILE
    tiles_per_seq = seq // TOK_TILE
    n_blk = m // BLK
    front = lambda s: jnp.minimum(s, n_tiles - 1)
    back = lambda s: jnp.maximum(s - 1, 0)
    tok_front = lambda s: (front(s), 0)
    tok_back = lambda s: (back(s), 0)
    prev = lambda s: (jnp.maximum(front(s) * ATT_QBLKS - 1, 0), 0)
    nxt = lambda s: (jnp.minimum((front(s) + 1) * ATT_QBLKS, n_blk - 1), 0)
    edge_spec = lambda imap: pl.BlockSpec((BLK, KV_DIM), imap)
    own_spec = pl.BlockSpec((TOK_TILE, KV_DIM), tok_front)
    layer0 = lambda shape: _const_spec((None,) + shape, (0,) * (len(shape) + 1))
    return pl.pallas_call(
        functools.partial(_attn_ffn_kernel, n_tiles=n_tiles, tiles_per_seq=tiles_per_seq),
        grid=(n_tiles + 1,),
        in_specs=[
            pl.BlockSpec(memory_space=pltpu.SMEM),
            pl.BlockSpec((TOK_TILE, Q_DIM), tok_front),
            edge_spec(prev), own_spec, edge_spec(nxt),
            edge_spec(prev), own_spec, edge_spec(nxt),
            pl.BlockSpec((TOK_TILE, D_MODEL), tok_back),
            _const_spec((Q_DIM, D_MODEL)),
            _const_spec((1, D_MODEL)),
            layer0((D_MODEL, 2 * D_FF)),
            layer0((D_FF, D_MODEL)),
        ],
        out_specs=pl.BlockSpec((TOK_TILE, D_MODEL), tok_back),
        out_shape=jax.ShapeDtypeStruct((m, D_MODEL), _f32),
        scratch_shapes=[
            pltpu.VMEM((N_KV_HEADS, ATT_KBLKS * PAIR_ROWS, LANES), _bf16),
            pltpu.VMEM((N_KV_HEADS, ATT_KBLKS * PAIR_ROWS, 2 * LANES), _bf16),
            pltpu.VMEM((TOK_TILE, Q_DIM), _bf16),
            pltpu.VMEM((ATT_UNITS, BLK, 3 * PAIR_ROWS), _bf16),
            pltpu.VMEM((ATT_UNITS, BLK, LANES), _f32),
            pltpu.VMEM((TOK_TILE, D_FF), _bf16),
        ],
        compiler_params=_compiler_params(1),
        name="attn_ffn",
    )(sink, q, k, k, k, v, v, v, x2, w_o, g, w_gu, w_d)


def _conv_ffn_kernel(xa_ref, xp_ref, xn_ref, xb_ref, gc_ref, w1_ref, b1_ref, wdw_ref, bdw_ref, lng_ref, lnb_ref,
                     w2_ref, b2_ref, gf_ref, wgu_ref, wd_ref, gfin_ref, out_ref,
                     u_ref, c_ref, y_ref, act_ref, *, n_tiles, tiles_per_seq):
    s = pl.program_id(0)
    rows = TOK_TILE + 2 * HALO

    @pl.when(s == 0)
    def _():
        y_ref[...] = jnp.zeros_like(y_ref)

    x1 = xb_ref[...] + jnp.dot(y_ref[...], w2_ref[...], preferred_element_type=_f32) + b2_ref[...]

    t_idx = jnp.minimum(s, n_tiles - 1)
    first = (t_idx % tiles_per_seq) == 0
    last = (t_idx % tiles_per_seq) == tiles_per_seq - 1
    xh = jnp.concatenate([xp_ref[...], xa_ref[...], xn_ref[...]], axis=0)
    h = _rmsnorm(xh, gc_ref[...]).astype(_bf16)
    u = jnp.dot(h, w1_ref[...], preferred_element_type=_f32) + b1_ref[...]
    u = u[:, :CONV_CH] * jax.nn.sigmoid(u[:, CONV_CH:])
    r = lax.broadcasted_iota(jnp.int32, (rows, 1), 0)
    outside = ((r < HALO) & first) | ((r >= HALO + TOK_TILE) & last)
    u = jnp.where(outside, 0.0, u)
    for t in range(N_LANE_TILES):
        u_ref[t] = u[:, t * LANES:(t + 1) * LANES]

    def conv_unit(t, rc):
        def run(zeros, row_lo, row_hi):
            acc = jnp.zeros((CONV_ROWS, LANES), _f32)
            for j in range(CONV_WIDTH):
                start = rc * CONV_ROWS + HALO - CONV_PAD + j
                gate_row = (row_lo + (j * (row_hi - row_lo)) // CONV_WIDTH) // SUBLANES * SUBLANES
                w_row = wdw_ref[t, pl.ds(j, 1), :] + zeros[gate_row:gate_row + 1, :]
                acc = acc + u_ref[t, pl.ds(start, CONV_ROWS), :] * w_row
            c_ref[t, pl.ds(rc * CONV_ROWS, CONV_ROWS), :] = acc + bdw_ref[t]
        return run

    def norm_unit(rc):
        def run(zeros, row_lo, row_hi):
            del zeros, row_lo, row_hi
            rsl = pl.ds(rc * CONV_ROWS, CONV_ROWS)
            c = jnp.concatenate([c_ref[t, rsl, :] for t in range(N_LANE_TILES)], axis=1)
            mu = jnp.mean(c, axis=-1, keepdims=True)
            var = jnp.mean(jnp.square(c - mu), axis=-1, keepdims=True)
            y = (c - mu) * lax.rsqrt(var + EPS) * lng_ref[...] + lnb_ref[...]
            y_ref[rsl, :] = (y * jax.nn.sigmoid(y)).astype(_bf16)
        return run

    n_rc = TOK_TILE // CONV_ROWS
    units = [conv_unit(t, rc) for t in range(N_LANE_TILES) for rc in range(n_rc)]
    units += [norm_unit(rc) for rc in range(n_rc)]

    x2 = _ffn_residual(x1, gf_ref[...], wgu_ref, wd_ref, act_ref, units, fillers_have_matmuls=False)
    out_ref[...] = _rmsnorm(x2, gfin_ref[...])


def _conv_ffn_call(x2, gc, w1, b1, wdw, bdw, lng, lnb, w2, b2, gf, w_gu, w_d, gfin, seq):
    m = x2.shape[0]
    n_tiles = m // TOK_TILE
    tiles_per_seq = seq // TOK_TILE
    halo_per_tile = TOK_TILE // HALO
    n_halo = m // HALO
    front = lambda s: jnp.minimum(s, n_tiles - 1)
    back = lambda s: jnp.maximum(s - 1, 0)
    tok_front = lambda s: (front(s), 0)
    tok_back = lambda s: (back(s), 0)
    prev = lambda s: (jnp.maximum(front(s) * halo_per_tile - 1, 0), 0)
    nxt = lambda s: (jnp.minimum((front(s) + 1) * halo_per_tile, n_halo - 1), 0)
    rows = TOK_TILE + 2 * HALO
    layer1 = lambda shape: _const_spec((None,) + shape, (1,) + (0,) * len(shape))
    return pl.pallas_call(
        functools.partial(_conv_ffn_kernel, n_tiles=n_tiles, tiles_per_seq=tiles_per_seq),
        grid=(n_tiles + 1,),
        in_specs=[
            pl.BlockSpec((TOK_TILE, D_MODEL), tok_front),
            pl.BlockSpec((HALO, D_MODEL), prev),
            pl.BlockSpec((HALO, D_MODEL), nxt),
            pl.BlockSpec((TOK_TILE, D_MODEL), tok_back),
            _const_spec((1, D_MODEL)),
            _const_spec((D_MODEL, 2 * CONV_CH)),
            _const_spec((1, 2 * CONV_CH)),
            _const_spec((N_LANE_TILES, CONV_WIDTH, LANES)),
            _const_spec((N_LANE_TILES, 1, LANES)),
            _const_spec((1, CONV_CH)),
            _const_spec((1, CONV_CH)),
            _const_spec((CONV_CH, D_MODEL)),
            _const_spec((1, D_MODEL)),
            _const_spec((1, D_MODEL)),
            layer1((D_MODEL, 2 * D_FF)),
            layer1((D_FF, D_MODEL)),
            _const_spec((1, D_MODEL)),
        ],
        out_specs=pl.BlockSpec((TOK_TILE, D_MODEL), tok_back),
        out_shape=jax.ShapeDtypeStruct((m, D_MODEL), _f32),
        scratch_shapes=[
            pltpu.VMEM((N_LANE_TILES, rows, LANES), _f32),
            pltpu.VMEM((N_LANE_TILES, TOK_TILE, LANES), _f32),
            pltpu.VMEM((TOK_TILE, CONV_CH), _bf16),
            pltpu.VMEM((TOK_TILE, D_FF), _bf16),
        ],
        compiler_params=_compiler_params(1),
        name="conv_ffn",
    )(x2, x2, x2, x2, gc, w1, b1, wdw, bdw, lng, lnb, w2, b2, gf, w_gu, w_d, gfin)


def _rope_tables(seq):
    pos = jnp.arange(seq, dtype=_f32)
    inv_freq = ROPE_THETA ** (-jnp.arange(0, ROT_DIM, 2, dtype=_f32) / ROT_DIM)
    ang = pos[:, None] * inv_freq[None, :]
    cos = jnp.cos(ang)
    sin = jnp.sin(ang)
    ones = jnp.ones((seq, HEAD_DIM - ROT_DIM), _f32)
    zeros_half = jnp.zeros((seq, ROT_HALF), _f32)
    zeros_rest = jnp.zeros((seq, HEAD_DIM - ROT_DIM), _f32)
    cos_h = jnp.concatenate([cos, cos, ones], axis=1)
    sin_hi_h = jnp.concatenate([zeros_half, sin, zeros_rest], axis=1)
    sin_lo_h = jnp.concatenate([-sin, zeros_half, zeros_rest], axis=1)
    rep = lambda a: jnp.tile(a, (1, HEADS_PER_TILE))
    return rep(cos_h), rep(sin_hi_h), rep(sin_lo_h)


def kernel(x, attn_norm, attn_w_qkv, attn_w_o, attn_sink, conv_norm, conv_w_pw1, conv_b_pw1, conv_w_dw,
           conv_b_dw, conv_ln_g, conv_ln_b, conv_w_pw2, conv_b_pw2, ffn_norm, ffn_w_gu, ffn_w_down, final_norm):
    b, s, d = x.shape
    assert d == D_MODEL and s % TOK_TILE == 0
    assert attn_norm.shape[0] == 1 and conv_norm.shape[0] == 1 and ffn_norm.shape[0] == 2
    m = b * s
    x2 = x.reshape(m, d)
    row = lambda v: v.reshape(1, -1).astype(_f32)
    bf = lambda w: w.astype(_bf16)
    w_gu = bf(ffn_w_gu)
    w_d = bf(ffn_w_down)

    cos_t, sin_hi_t, sin_lo_t = _rope_tables(s)
    q, k, v = _qkv_call(x2, row(attn_norm[0]), bf(attn_w_qkv[0]), cos_t, sin_hi_t, sin_lo_t, s)
    x2 = _attn_ffn_call(x2, q, k, v, attn_sink[0].astype(_f32), bf(attn_w_o[0]), row(ffn_norm[0]),
                        w_gu, w_d, s)

    wdw = conv_w_dw[0].reshape(CONV_WIDTH, N_LANE_TILES, LANES).transpose(1, 0, 2)
    bdw = conv_b_dw[0].reshape(N_LANE_TILES, 1, LANES)
    out = _conv_ffn_call(x2, row(conv_norm[0]), bf(conv_w_pw1[0]), row(conv_b_pw1[0]), wdw, bdw,
                         row(conv_ln_g[0]), row(conv_ln_b[0]), bf(conv_w_pw2[0]), row(conv_b_pw2[0]),
                         row(ffn_norm[1]), w_gu, w_d, row(final_norm), s)
    return out.reshape(b, s, d)
```

```python
import functools
import math

import jax
import jax.numpy as jnp
from jax import lax
from jax.experimental import pallas as pl
from jax.experimental.pallas import tpu as pltpu

D_MODEL = 1024
N_HEADS = 16
N_KV_HEADS = 4
HEAD_DIM = D_MODEL // N_HEADS
GROUP = N_HEADS // N_KV_HEADS
ROT_DIM = HEAD_DIM // 4
ROT_HALF = ROT_DIM // 2
ROPE_THETA = 500000.0
WINDOW = 128
BLK = 128
Q_DIM = N_HEADS * HEAD_DIM
KV_DIM = N_KV_HEADS * HEAD_DIM
QKV_DIM = Q_DIM + 2 * KV_DIM
CONV_CH = D_MODEL
CONV_WIDTH = 31
CONV_PAD = (CONV_WIDTH - 1) // 2
D_FF = ((8 * D_MODEL // 3 + 255) // 256) * 256
EPS = 1e-6
NEG = -1e30

LANES = 128
SUBLANES = 8
VMEM_LIMIT_BYTES = 56 * 1024 * 1024

TOK_TILE = 512
FF_CHUNK = 256
HALO = 2 * SUBLANES
CONV_ROWS = 64
N_LANE_TILES = D_MODEL // LANES
HEADS_PER_TILE = LANES // HEAD_DIM
PAIRS_PER_GROUP = GROUP // HEADS_PER_TILE
ATT_QBLKS = TOK_TILE // BLK
ATT_KBLKS = ATT_QBLKS + 2
ATT_VALUE_LAG = 2
ATT_P_SLOTS = 2 * (ATT_VALUE_LAG + 1)
PAIR_ROWS = HEADS_PER_TILE * BLK

_f32 = jnp.float32
_bf16 = jnp.bfloat16


def _rmsnorm(xf, g):
    y = xf * lax.rsqrt(jnp.mean(xf * xf, axis=-1, keepdims=True) + EPS)
    return y * g


def _const_spec(shape, index=None):
    index = (0,) * len(shape) if index is None else index
    return pl.BlockSpec(shape, lambda *_: index, pipeline_mode=pl.Buffered(1))


def _compiler_params(n_axes):
    return pltpu.CompilerParams(
        dimension_semantics=("arbitrary",) * n_axes,
        vmem_limit_bytes=VMEM_LIMIT_BYTES,
    )


def _qkv_kernel(x_ref, g_ref, w_ref, cos_ref, sin_hi_ref, sin_lo_ref, q_ref, k_ref, v_ref):
    h = _rmsnorm(x_ref[...], g_ref[...]).astype(_bf16)
    qkv = jnp.dot(h, w_ref[...], preferred_element_type=_f32)
    cos = cos_ref[...]
    sin_hi = sin_hi_ref[...]
    sin_lo = sin_lo_ref[...]
    scale = 1.0 / math.sqrt(HEAD_DIM)

    def rope(t):
        return (t * cos + pltpu.roll(t, ROT_HALF, 1) * sin_hi
                + pltpu.roll(t, LANES - ROT_HALF, 1) * sin_lo)

    for j in range(Q_DIM // LANES):
        t = qkv[:, j * LANES:(j + 1) * LANES]
        q_ref[:, j * LANES:(j + 1) * LANES] = (rope(t) * scale).astype(_bf16)
    for j in range(KV_DIM // LANES):
        t = qkv[:, Q_DIM + j * LANES:Q_DIM + (j + 1) * LANES]
        k_ref[:, j * LANES:(j + 1) * LANES] = rope(t).astype(_bf16)
    v_ref[...] = qkv[:, Q_DIM + KV_DIM:].astype(_bf16)


def _qkv_call(x2, g, w_qkv, cos_t, sin_hi_t, sin_lo_t, seq):
    m = x2.shape[0]
    tiles_per_seq = seq // TOK_TILE
    tok = lambda i: (i, 0)
    pos = lambda i: (i % tiles_per_seq, 0)
    return pl.pallas_call(
        _qkv_kernel,
        grid=(m // TOK_TILE,),
        in_specs=[
            pl.BlockSpec((TOK_TILE, D_MODEL), tok),
            _const_spec((1, D_MODEL)),
            _const_spec((D_MODEL, QKV_DIM)),
            pl.BlockSpec((TOK_TILE, LANES), pos),
            pl.BlockSpec((TOK_TILE, LANES), pos),
            pl.BlockSpec((TOK_TILE, LANES), pos),
        ],
        out_specs=[
            pl.BlockSpec((TOK_TILE, Q_DIM), tok),
            pl.BlockSpec((TOK_TILE, KV_DIM), tok),
            pl.BlockSpec((TOK_TILE, KV_DIM), tok),
        ],
        out_shape=[
            jax.ShapeDtypeStruct((m, Q_DIM), _bf16),
            jax.ShapeDtypeStruct((m, KV_DIM), _bf16),
            jax.ShapeDtypeStruct((m, KV_DIM), _bf16),
        ],
        compiler_params=_compiler_params(1),
        name="qkv_rope",
    )(x2, g, w_qkv, cos_t, sin_hi_t, sin_lo_t)


def _spread(thunks, n_slots):
    bounds = [(len(thunks) * i) // n_slots for i in range(n_slots + 1)]
    return [thunks[bounds[i]:bounds[i + 1]] for i in range(n_slots)]


def _run_fillers(group, produced):
    cache = {}

    def zero_row(r):
        r = r // SUBLANES * SUBLANES
        if r not in cache:
            cache[r] = jnp.minimum(jnp.abs(produced[r:r + SUBLANES, :LANES]), 0.0)
        return cache[r][:1, :]

    n_rows = produced.shape[0]
    for i, thunk in enumerate(group):
        thunk(zero_row, (i * n_rows) // len(group), ((i + 1) * n_rows) // len(group))


def _run_ungated(group):
    zero = jnp.zeros((1, LANES), _f32)
    for thunk in group:
        thunk(lambda r: zero, 0, 0)


def _ffn_residual(xf, g, wgu_ref, wd_ref, act_ref, fillers, fillers_have_matmuls):
    n_up = D_FF // FF_CHUNK
    n_down = D_MODEL // FF_CHUNK
    groups = _spread(list(fillers), n_up + n_down)
    h = _rmsnorm(xf, g).astype(_bf16)
    previous = xf
    for c in range(n_up):
        lo = c * FF_CHUNK
        gate = jnp.dot(h, wgu_ref[:, lo:lo + FF_CHUNK], preferred_element_type=_f32)
        up = jnp.dot(h, wgu_ref[:, D_FF + lo:D_FF + lo + FF_CHUNK], preferred_element_type=_f32)
        act = gate * jax.nn.sigmoid(gate) * up
        act_ref[:, lo:lo + FF_CHUNK] = act.astype(_bf16)
        _run_fillers(groups[c], previous if fillers_have_matmuls else act)
        previous = act
    outs = []
    for c in range(n_down):
        lo = c * FF_CHUNK
        out = xf[:, lo:lo + FF_CHUNK] + jnp.dot(act_ref[...], wd_ref[:, lo:lo + FF_CHUNK],
                                                 preferred_element_type=_f32)
        outs.append(out)
        _run_fillers(groups[n_up + c], previous if fillers_have_matmuls else out)
        previous = out
    return jnp.concatenate(outs, axis=1)


def _stage_kv_stacks(kp_ref, kc_ref, kn_ref, vp_ref, vc_ref, vn_ref, kz_ref, vz_ref):
    lane = lax.broadcasted_iota(jnp.int32, (BLK, LANES), 1)
    low_lanes = lane < HEAD_DIM
    zero = jnp.zeros((), _bf16)
    high_flag = (lane // HEAD_DIM).astype(_f32)
    ones_low = (1.0 - high_flag).astype(_bf16)
    ones_high = high_flag.astype(_bf16)
    for kb in range(ATT_KBLKS):
        if kb == 0:
            k_blk, v_blk = kp_ref[...], vp_ref[...]
        elif kb == ATT_KBLKS - 1:
            k_blk, v_blk = kn_ref[...], vn_ref[...]
        else:
            rows = slice((kb - 1) * BLK, kb * BLK)
            k_blk, v_blk = kc_ref[rows, :], vc_ref[rows, :]
        for g in range(N_KV_HEADS):
            tile = slice((g // HEADS_PER_TILE) * LANES, (g // HEADS_PER_TILE + 1) * LANES)
            own = low_lanes if g % HEADS_PER_TILE == 0 else ~low_lanes
            k_own = jnp.where(own, k_blk[:, tile], zero)
            v_own = jnp.where(own, v_blk[:, tile], zero)
            k_swap = pltpu.roll(k_own, HEAD_DIM, 1)
            v_swap = pltpu.roll(v_own, HEAD_DIM, 1)
            if g % HEADS_PER_TILE == 0:
                k_lo, k_hi, v_lo, v_hi = k_own, k_swap, v_own, v_swap
            else:
                k_lo, k_hi, v_lo, v_hi = k_swap, k_own, v_swap, v_own
            base = kb * PAIR_ROWS
            kz_ref[g, base:base + BLK, :] = k_lo
            kz_ref[g, base + BLK:base + PAIR_ROWS, :] = k_hi
            vz_ref[g, base:base + BLK, :LANES] = v_lo
            vz_ref[g, base + BLK:base + PAIR_ROWS, :LANES] = v_hi
            vz_ref[g, base:base + BLK, LANES:] = ones_low
            vz_ref[g, base + BLK:base + PAIR_ROWS, LANES:] = ones_high


def _attn_group_unit(j, g, prev_ok, next_ok, low_lanes_f32, sink_ref, q_ref, kz_ref, vz_ref, o_ref,
                     p_ref, sink_term_ref):
    slot = (j * N_KV_HEADS + g) % ATT_P_SLOTS
    rows = slice(j * BLK, (j + 1) * BLK)
    stack = slice(j * PAIR_ROWS, (j + 3) * PAIR_ROWS)
    q_tiles = [g * PAIRS_PER_GROUP + pair for pair in range(PAIRS_PER_GROUP)]

    def scores(zero_row, row_lo, row_hi):
        del row_hi
        q2 = jnp.concatenate([q_ref[rows, qt * LANES:(qt + 1) * LANES] for qt in q_tiles], axis=0)
        q2 = q2 + zero_row(row_lo).astype(_bf16)
        s_all = lax.dot_general(q2, kz_ref[g, stack, :], (((1,), (1,)), ((), ())),
                                preferred_element_type=_f32)
        for pair, qt in enumerate(q_tiles):
            s = s_all[pair * BLK:(pair + 1) * BLK, :]
            probs = [None] * (3 * HEADS_PER_TILE)
            sink_terms = []
            for half in range(HEADS_PER_TILE):
                sink = sink_ref[qt * HEADS_PER_TILE + half]
                col = lambda kb: slice((kb * HEADS_PER_TILE + half) * BLK, (kb * HEADS_PER_TILE + half + 1) * BLK)
                s_prev = jnp.where(prev_ok, s[:, col(0)], NEG)
                s_own = s[:, col(1)]
                s_next = jnp.where(next_ok, s[:, col(2)], NEG)
                mx = jnp.max(jnp.maximum(jnp.maximum(s_prev, s_own), s_next), axis=-1, keepdims=True)
                mx = jnp.maximum(mx, sink)
                probs[0 * HEADS_PER_TILE + half] = jnp.exp(s_prev - mx).astype(_bf16)
                probs[1 * HEADS_PER_TILE + half] = jnp.exp(s_own - mx).astype(_bf16)
                probs[2 * HEADS_PER_TILE + half] = jnp.exp(s_next - mx).astype(_bf16)
                sink_terms.append(jnp.exp(sink - mx))
            p_ref[slot, pair * BLK:(pair + 1) * BLK, :] = jnp.concatenate(probs, axis=1)
            sink_term_ref[slot, pair * BLK:(pair + 1) * BLK, :] = jnp.where(low_lanes_f32, sink_terms[0],
                                                                            sink_terms[1])

    def values(zero_row, row_lo, row_hi):
        del zero_row, row_lo, row_hi
        o_ext = jnp.dot(p_ref[slot], vz_ref[g, stack, :], preferred_element_type=_f32)
        o_all = o_ext[:, :LANES] / (o_ext[:, LANES:] + sink_term_ref[slot])
        for pair, qt in enumerate(q_tiles):
            o_ref[rows, qt * LANES:(qt + 1) * LANES] = o_all[pair * BLK:(pair + 1) * BLK, :].astype(_bf16)

    return scores, values


def _attn_ffn_kernel(sink_ref, q_ref, kp_ref, kc_ref, kn_ref, vp_ref, vc_ref, vn_ref, xb_ref,
                     wo_ref, g_ref, wgu_ref, wd_ref, gc_ref, w1_ref, b1_ref, out_ref, u_out_ref,
                     kz_ref, vz_ref, o_ref, p_ref, sink_term_ref, act_ref, *, n_tiles, tiles_per_seq):
    s = pl.program_id(0)

    def front_units():
        _stage_kv_stacks(kp_ref, kc_ref, kn_ref, vp_ref, vc_ref, vn_ref, kz_ref, vz_ref)
        has_prev = (s % tiles_per_seq) != 0
        has_next = (s % tiles_per_seq) != tiles_per_seq - 1
        qi = lax.broadcasted_iota(jnp.int32, (BLK, BLK), 0)
        ci = lax.broadcasted_iota(jnp.int32, (BLK, BLK), 1)
        prev_in_window = ci >= qi
        next_in_window = ci <= qi
        low_lanes_f32 = (lax.broadcasted_iota(jnp.int32, (BLK, LANES), 1) & HEAD_DIM) == 0
        halves = []
        for j in range(ATT_QBLKS):
            prev_ok = prev_in_window & has_prev if j == 0 else prev_in_window
            next_ok = next_in_window & has_next if j == ATT_QBLKS - 1 else next_in_window
            for g in range(N_KV_HEADS):
                halves.append(_attn_group_unit(j, g, prev_ok, next_ok, low_lanes_f32, sink_ref, q_ref,
                                               kz_ref, vz_ref, o_ref, p_ref, sink_term_ref))
        units = []
        for i in range(len(halves) + ATT_VALUE_LAG):
            if i < len(halves):
                units.append(halves[i][0])
            if i >= ATT_VALUE_LAG:
                units.append(halves[i - ATT_VALUE_LAG][1])
        return units

    def back_half(units):
        x1 = xb_ref[...] + jnp.dot(o_ref[...], wo_ref[...], preferred_element_type=_f32)
        units = units() if callable(units) else units
        x2 = _ffn_residual(x1, g_ref[...], wgu_ref, wd_ref, act_ref, units, fillers_have_matmuls=True)
        out_ref[...] = x2
        h = _rmsnorm(x2, gc_ref[...]).astype(_bf16)
        for c in range(CONV_CH // FF_CHUNK):
            lo = c * FF_CHUNK
            val = (jnp.dot(h, w1_ref[:, lo:lo + FF_CHUNK], preferred_element_type=_f32)
                   + b1_ref[:, lo:lo + FF_CHUNK])
            gate = (jnp.dot(h, w1_ref[:, CONV_CH + lo:CONV_CH + lo + FF_CHUNK], preferred_element_type=_f32)
                    + b1_ref[:, CONV_CH + lo:CONV_CH + lo + FF_CHUNK])
            u_out_ref[:, lo:lo + FF_CHUNK] = val * jax.nn.sigmoid(gate)

    @pl.when(s == 0)
    def _():
        _run_ungated(front_units())

    @pl.when((s > 0) & (s < n_tiles))
    def _():
        back_half(front_units)

    @pl.when(s == n_tiles)
    def _():
        back_half([])


def _attn_ffn_call(x2, q, k, v, sink, w_o, g, w_gu, w_d, gc, w1, b1, seq):
    m = x2.shape[0]
    n_tiles = m // TOK_TILE
    tiles_per_seq = seq // TOK_TILE
    n_blk = m // BLK
    front = lambda s: jnp.minimum(s, n_tiles - 1)
    back = lambda s: jnp.maximum(s - 1, 0)
    tok_front = lambda s: (front(s), 0)
    tok_back = lambda s: (back(s), 0)
    prev = lambda s: (jnp.maximum(front(s) * ATT_QBLKS - 1, 0), 0)
    nxt = lambda s: (jnp.minimum((front(s) + 1) * ATT_QBLKS, n_blk - 1), 0)
    edge_spec = lambda imap: pl.BlockSpec((BLK, KV_DIM), imap)
    own_spec = pl.BlockSpec((TOK_TILE, KV_DIM), tok_front)
    layer0 = lambda shape: _const_spec((None,) + shape, (0,) * (len(shape) + 1))
    return pl.pallas_call(
        functools.partial(_attn_ffn_kernel, n_tiles=n_tiles, tiles_per_seq=tiles_per_seq),
        grid=(n_tiles + 1,),
        in_specs=[
            pl.BlockSpec(memory_space=pltpu.SMEM),
            pl.BlockSpec((TOK_TILE, Q_DIM), tok_front),
            edge_spec(prev), own_spec, edge_spec(nxt),
            edge_spec(prev), own_spec, edge_spec(nxt),
            pl.BlockSpec((TOK_TILE, D_MODEL), tok_back),
            _const_spec((Q_DIM, D_MODEL)),
            _const_spec((1, D_MODEL)),
            layer0((D_MODEL, 2 * D_FF)),
            layer0((D_FF, D_MODEL)),
            _const_spec((1, D_MODEL)),
            _const_spec((D_MODEL, 2 * CONV_CH)),
            _const_spec((1, 2 * CONV_CH)),
        ],
        out_specs=[
            pl.BlockSpec((TOK_TILE, D_MODEL), tok_back),
            pl.BlockSpec((TOK_TILE, CONV_CH), tok_back),
        ],
        out_shape=[
            jax.ShapeDtypeStruct((m, D_MODEL), _f32),
            jax.ShapeDtypeStruct((m, CONV_CH), _f32),
        ],
        scratch_shapes=[
            pltpu.VMEM((N_KV_HEADS, ATT_KBLKS * PAIR_ROWS, LANES), _bf16),
            pltpu.VMEM((N_KV_HEADS, ATT_KBLKS * PAIR_ROWS, 2 * LANES), _bf16),
            pltpu.VMEM((TOK_TILE, Q_DIM), _bf16),
            pltpu.VMEM((ATT_P_SLOTS, PAIRS_PER_GROUP * BLK, 3 * PAIR_ROWS), _bf16),
            pltpu.VMEM((ATT_P_SLOTS, PAIRS_PER_GROUP * BLK, LANES), _f32),
            pltpu.VMEM((TOK_TILE, D_FF), _bf16),
        ],
        compiler_params=_compiler_params(1),
        name="attn_ffn",
    )(sink, q, k, k, k, v, v, v, x2, w_o, g, w_gu, w_d, gc, w1, b1)


def _conv_ffn_kernel(ua_ref, up_ref, un_ref, xb_ref, wdw_ref, bdw_ref, lng_ref, lnb_ref,
                     w2_ref, b2_ref, gf_ref, wgu_ref, wd_ref, gfin_ref, out_ref,
                     u_ref, c_ref, y_ref, act_ref, *, n_tiles, tiles_per_seq):
    s = pl.program_id(0)

    @pl.when(s == 0)
    def _():
        y_ref[...] = jnp.zeros_like(y_ref)

    t_idx = jnp.minimum(s, n_tiles - 1)
    has_prev = (t_idx % tiles_per_seq) != 0
    has_next = (t_idx % tiles_per_seq) != tiles_per_seq - 1
    u_prev = jnp.where(has_prev, up_ref[...], 0.0)
    u_next = jnp.where(has_next, un_ref[...], 0.0)
    for t in range(N_LANE_TILES):
        tile = slice(t * LANES, (t + 1) * LANES)
        u_ref[t, :HALO, :] = u_prev[:, tile]
        u_ref[t, HALO:HALO + TOK_TILE, :] = ua_ref[:, tile]
        u_ref[t, HALO + TOK_TILE:, :] = u_next[:, tile]

    def conv_unit(t, rc):
        def run(zero_row, row_lo, row_hi):
            acc = None
            for j in range(CONV_WIDTH):
                start = rc * CONV_ROWS + HALO - CONV_PAD + j
                w_rows = wdw_ref[t, j] + zero_row(row_lo + (j * (row_hi - row_lo)) // CONV_WIDTH)
                taps = u_ref[t, pl.ds(start, CONV_ROWS), :].reshape(CONV_ROWS // SUBLANES, SUBLANES, LANES)
                prod = taps * w_rows[None]
                acc = prod if acc is None else acc + prod
            c_ref[t, pl.ds(rc * CONV_ROWS, CONV_ROWS), :] = acc.reshape(CONV_ROWS, LANES) + bdw_ref[t]
        return run

    def norm_unit(rc):
        def run(zero_row, row_lo, row_hi):
            del zero_row, row_lo, row_hi
            rsl = pl.ds(rc * CONV_ROWS, CONV_ROWS)
            c = jnp.concatenate([c_ref[t, rsl, :] for t in range(N_LANE_TILES)], axis=1)
            mu = jnp.mean(c, axis=-1, keepdims=True)
            var = jnp.mean(jnp.square(c - mu), axis=-1, keepdims=True)
            y = (c - mu) * lax.rsqrt(var + EPS) * lng_ref[...] + lnb_ref[...]
            y_ref[rsl, :] = (y * jax.nn.sigmoid(y)).astype(_bf16)
        return run

    n_rc = TOK_TILE // CONV_ROWS
    units = [conv_unit(t, rc) for t in range(N_LANE_TILES) for rc in range(n_rc)]
    units += [norm_unit(rc) for rc in range(n_rc)]

    n_pw2 = D_MODEL // FF_CHUNK
    n_ffn = D_FF // FF_CHUNK + D_MODEL // FF_CHUNK
    n_early = (len(units) * n_pw2) // (n_pw2 + n_ffn)
    pw2_groups = _spread(units[:n_early], n_pw2)
    x1_cols = []
    for c in range(n_pw2):
        lo = c * FF_CHUNK
        x1_c = (xb_ref[:, lo:lo + FF_CHUNK]
                + jnp.dot(y_ref[...], w2_ref[:, lo:lo + FF_CHUNK], preferred_element_type=_f32)
                + b2_ref[:, lo:lo + FF_CHUNK])
        x1_cols.append(x1_c)
        _run_fillers(pw2_groups[c], x1_c)
    x1 = jnp.concatenate(x1_cols, axis=1)
    x2 = _ffn_residual(x1, gf_ref[...], wgu_ref, wd_ref, act_ref, units[n_early:], fillers_have_matmuls=False)
    out_ref[...] = _rmsnorm(x2, gfin_ref[...])


def _conv_ffn_call(x2, u, wdw, bdw, lng, lnb, w2, b2, gf, w_gu, w_d, gfin, seq):
    m = x2.shape[0]
    n_tiles = m // TOK_TILE
    tiles_per_seq = seq // TOK_TILE
    halo_per_tile = TOK_TILE // HALO
    n_halo = m // HALO
    front = lambda s: jnp.minimum(s, n_tiles - 1)
    back = lambda s: jnp.maximum(s - 1, 0)
    tok_front = lambda s: (front(s), 0)
    tok_back = lambda s: (back(s), 0)
    prev = lambda s: (jnp.maximum(front(s) * halo_per_tile - 1, 0), 0)
    nxt = lambda s: (jnp.minimum((front(s) + 1) * halo_per_tile, n_halo - 1), 0)
    rows = TOK_TILE + 2 * HALO
    layer1 = lambda shape: _const_spec((None,) + shape, (1,) + (0,) * len(shape))
    return pl.pallas_call(
        functools.partial(_conv_ffn_kernel, n_tiles=n_tiles, tiles_per_seq=tiles_per_seq),
        grid=(n_tiles + 1,),
        in_specs=[
            pl.BlockSpec((TOK_TILE, CONV_CH), tok_front),
            pl.BlockSpec((HALO, CONV_CH), prev),
            pl.BlockSpec((HALO, CONV_CH), nxt),
            pl.BlockSpec((TOK_TILE, D_MODEL), tok_back),
            _const_spec((N_LANE_TILES, CONV_WIDTH, SUBLANES, LANES)),
            _const_spec((N_LANE_TILES, 1, LANES)),
            _const_spec((1, CONV_CH)),
            _const_spec((1, CONV_CH)),
            _const_spec((CONV_CH, D_MODEL)),
            _const_spec((1, D_MODEL)),
            _const_spec((1, D_MODEL)),
            layer1((D_MODEL, 2 * D_FF)),
            layer1((D_FF, D_MODEL)),
            _const_spec((1, D_MODEL)),
        ],
        out_specs=pl.BlockSpec((TOK_TILE, D_MODEL), tok_back),
        out_shape=jax.ShapeDtypeStruct((m, D_MODEL), _f32),
        scratch_shapes=[
            pltpu.VMEM((N_LANE_TILES, rows, LANES), _f32),
            pltpu.VMEM((N_LANE_TILES, TOK_TILE, LANES), _f32),
            pltpu.VMEM((TOK_TILE, CONV_CH), _bf16),
            pltpu.VMEM((TOK_TILE, D_FF), _bf16),
        ],
        compiler_params=_compiler_params(1),
        name="conv_ffn",
    )(u, u, u, x2, wdw, bdw, lng, lnb, w2, b2, gf, w_gu, w_d, gfin)


def _rope_tables(seq):
    pos = jnp.arange(seq, dtype=_f32)
    inv_freq = ROPE_THETA ** (-jnp.arange(0, ROT_DIM, 2, dtype=_f32) / ROT_DIM)
    ang = pos[:, None] * inv_freq[None, :]
    cos = jnp.cos(ang)
    sin = jnp.sin(ang)
    ones = jnp.ones((seq, HEAD_DIM - ROT_DIM), _f32)
    zeros_half = jnp.zeros((seq, ROT_HALF), _f32)
    zeros_rest = jnp.zeros((seq, HEAD_DIM - ROT_DIM), _f32)
    cos_h = jnp.concatenate([cos, cos, ones], axis=1)
    sin_hi_h = jnp.concatenate([zeros_half, sin, zeros_rest], axis=1)
    sin_lo_h = jnp.concatenate([-sin, zeros_half, zeros_rest], axis=1)
    rep = lambda a: jnp.tile(a, (1, HEADS_PER_TILE))
    return rep(cos_h), rep(sin_hi_h), rep(sin_lo_h)


def kernel(x, attn_norm, attn_w_qkv, attn_w_o, attn_sink, conv_norm, conv_w_pw1, conv_b_pw1, conv_w_dw,
           conv_b_dw, conv_ln_g, conv_ln_b, conv_w_pw2, conv_b_pw2, ffn_norm, ffn_w_gu, ffn_w_down, final_norm):
    b, s, d = x.shape
    assert d == D_MODEL and s % TOK_TILE == 0
    assert attn_norm.shape[0] == 1 and conv_norm.shape[0] == 1 and ffn_norm.shape[0] == 2
    m = b * s
    x2 = x.reshape(m, d)
    row = lambda v: v.reshape(1, -1).astype(_f32)
    bf = lambda w: w.astype(_bf16)
    w_gu = bf(ffn_w_gu)
    w_d = bf(ffn_w_down)

    cos_t, sin_hi_t, sin_lo_t = _rope_tables(s)
    q, k, v = _qkv_call(x2, row(attn_norm[0]), bf(attn_w_qkv[0]), cos_t, sin_hi_t, sin_lo_t, s)
    x2, u = _attn_ffn_call(x2, q, k, v, attn_sink[0].astype(_f32), bf(attn_w_o[0]), row(ffn_norm[0]),
                           w_gu, w_d, row(conv_norm[0]), bf(conv_w_pw1[0]), row(conv_b_pw1[0]), s)

    wdw = conv_w_dw[0].reshape(CONV_WIDTH, N_LANE_TILES, LANES).transpose(1, 0, 2)
    wdw = jnp.broadcast_to(wdw[:, :, None, :], (N_LANE_TILES, CONV_WIDTH, SUBLANES, LANES))
    bdw = conv_b_dw[0].reshape(N_LANE_TILES, 1, LANES)
    out = _conv_ffn_call(x2, u, wdw, bdw, row(conv_ln_g[0]), row(conv_ln_b[0]), bf(conv_w_pw2[0]),
                         row(conv_b_pw2[0]), row(ffn_norm[1]), w_gu, w_d, row(final_norm), s)
    return out.reshape(b, s, d)
```

```python
import functools
import math

import jax
import jax.numpy as jnp
import numpy as np
from jax import lax
from jax.experimental import pallas as pl
from jax.experimental.pallas import tpu as pltpu

D_MODEL = 1024
N_HEADS = 16
N_KV_HEADS = 4
HEAD_DIM = D_MODEL // N_HEADS
GROUP = N_HEADS // N_KV_HEADS
ROT_DIM = HEAD_DIM // 4
ROT_HALF = ROT_DIM // 2
ROPE_THETA = 500000.0
WINDOW = 128
BLK = 128
Q_DIM = N_HEADS * HEAD_DIM
KV_DIM = N_KV_HEADS * HEAD_DIM
QKV_DIM = Q_DIM + 2 * KV_DIM
CONV_CH = D_MODEL
CONV_WIDTH = 31
CONV_PAD = (CONV_WIDTH - 1) // 2
D_FF = ((8 * D_MODEL // 3 + 255) // 256) * 256
EPS = 1e-6
NEG = -1e30

LANES = 128
SUBLANES = 8
VMEM_LIMIT_BYTES = 56 * 1024 * 1024

TOK_TILE = 512
FF_CHUNK = 256
HALO = 2 * SUBLANES
CONV_ROWS = 64
N_LANE_TILES = D_MODEL // LANES
HEADS_PER_TILE = LANES // HEAD_DIM
PAIRS_PER_GROUP = GROUP // HEADS_PER_TILE
ATT_QBLKS = TOK_TILE // BLK
ATT_KBLKS = ATT_QBLKS + 2
ATT_VALUE_LAG = 2
ATT_P_SLOTS = 2 * (ATT_VALUE_LAG + 1)
PAIR_ROWS = HEADS_PER_TILE * BLK

_f32 = jnp.float32
_bf16 = jnp.bfloat16


def _rmsnorm(xf, g):
    y = xf * lax.rsqrt(jnp.mean(xf * xf, axis=-1, keepdims=True) + EPS)
    return y * g


def _const_spec(shape, index=None):
    index = (0,) * len(shape) if index is None else index
    return pl.BlockSpec(shape, lambda *_: index, pipeline_mode=pl.Buffered(1))


def _compiler_params(n_axes):
    return pltpu.CompilerParams(
        dimension_semantics=("arbitrary",) * n_axes,
        vmem_limit_bytes=VMEM_LIMIT_BYTES,
    )


def _qkv_kernel(x_ref, g_ref, w_ref, cos_ref, sin_hi_ref, sin_lo_ref, q_ref, k_ref, v_ref):
    h = _rmsnorm(x_ref[...], g_ref[...]).astype(_bf16)
    qkv = jnp.dot(h, w_ref[...], preferred_element_type=_f32)
    cos = cos_ref[...]
    sin_hi = sin_hi_ref[...]
    sin_lo = sin_lo_ref[...]
    scale = 1.0 / math.sqrt(HEAD_DIM)

    def rope(t):
        return (t * cos + pltpu.roll(t, ROT_HALF, 1) * sin_hi
                + pltpu.roll(t, LANES - ROT_HALF, 1) * sin_lo)

    for j in range(Q_DIM // LANES):
        t = qkv[:, j * LANES:(j + 1) * LANES]
        q_ref[:, j * LANES:(j + 1) * LANES] = (rope(t) * scale).astype(_bf16)
    for j in range(KV_DIM // LANES):
        t = qkv[:, Q_DIM + j * LANES:Q_DIM + (j + 1) * LANES]
        k_ref[:, j * LANES:(j + 1) * LANES] = rope(t).astype(_bf16)
    v_ref[...] = qkv[:, Q_DIM + KV_DIM:].astype(_bf16)


def _qkv_call(x2, g, w_qkv, cos_t, sin_hi_t, sin_lo_t, seq):
    m = x2.shape[0]
    tiles_per_seq = seq // TOK_TILE
    tok = lambda i: (i, 0)
    pos = lambda i: (i % tiles_per_seq, 0)
    return pl.pallas_call(
        _qkv_kernel,
        grid=(m // TOK_TILE,),
        in_specs=[
            pl.BlockSpec((TOK_TILE, D_MODEL), tok),
            _const_spec((1, D_MODEL)),
            _const_spec((D_MODEL, QKV_DIM)),
            pl.BlockSpec((TOK_TILE, LANES), pos),
            pl.BlockSpec((TOK_TILE, LANES), pos),
            pl.BlockSpec((TOK_TILE, LANES), pos),
        ],
        out_specs=[
            pl.BlockSpec((TOK_TILE, Q_DIM), tok),
            pl.BlockSpec((TOK_TILE, KV_DIM), tok),
            pl.BlockSpec((TOK_TILE, KV_DIM), tok),
        ],
        out_shape=[
            jax.ShapeDtypeStruct((m, Q_DIM), _bf16),
            jax.ShapeDtypeStruct((m, KV_DIM), _bf16),
            jax.ShapeDtypeStruct((m, KV_DIM), _bf16),
        ],
        compiler_params=_compiler_params(1),
        name="qkv_rope",
    )(x2, g, w_qkv, cos_t, sin_hi_t, sin_lo_t)


def _spread(thunks, n_slots):
    bounds = [(len(thunks) * i) // n_slots for i in range(n_slots + 1)]
    return [thunks[bounds[i]:bounds[i + 1]] for i in range(n_slots)]


def _run_fillers(group, produced):
    cache = {}

    def zero_row(r):
        r = r // SUBLANES * SUBLANES
        if r not in cache:
            cache[r] = jnp.minimum(jnp.abs(produced[r:r + SUBLANES, :LANES]), 0.0)
        return cache[r][:1, :]

    n_rows = produced.shape[0]
    for i, thunk in enumerate(group):
        thunk(zero_row, (i * n_rows) // len(group), ((i + 1) * n_rows) // len(group))


def _run_ungated(group):
    zero = jnp.zeros((1, LANES), _f32)
    for thunk in group:
        thunk(lambda r: zero, 0, 0)


def _ffn_residual(xf, g, wgu_ref, wd_ref, act_ref, fillers, fillers_have_matmuls):
    n_up = D_FF // FF_CHUNK
    n_down = D_MODEL // FF_CHUNK
    groups = _spread(list(fillers), n_up + n_down)
    h = _rmsnorm(xf, g).astype(_bf16)
    previous = xf
    for c in range(n_up):
        lo = c * FF_CHUNK
        gate = jnp.dot(h, wgu_ref[:, lo:lo + FF_CHUNK], preferred_element_type=_f32)
        up = jnp.dot(h, wgu_ref[:, D_FF + lo:D_FF + lo + FF_CHUNK], preferred_element_type=_f32)
        act = gate * jax.nn.sigmoid(gate) * up
        act_ref[:, lo:lo + FF_CHUNK] = act.astype(_bf16)
        _run_fillers(groups[c], previous if fillers_have_matmuls else act)
        previous = act
    outs = []
    for c in range(n_down):
        lo = c * FF_CHUNK
        out = xf[:, lo:lo + FF_CHUNK] + jnp.dot(act_ref[...], wd_ref[:, lo:lo + FF_CHUNK],
                                                 preferred_element_type=_f32)
        outs.append(out)
        _run_fillers(groups[n_up + c], previous if fillers_have_matmuls else out)
        previous = out
    return jnp.concatenate(outs, axis=1)


def _stage_kv_stacks(kp_ref, kc_ref, kn_ref, vp_ref, vc_ref, vn_ref, kz_ref, vz_ref):
    lane = lax.broadcasted_iota(jnp.int32, (BLK, LANES), 1)
    low_lanes = lane < HEAD_DIM
    zero = jnp.zeros((), _bf16)
    high_flag = (lane // HEAD_DIM).astype(_f32)
    ones_low = (1.0 - high_flag).astype(_bf16)
    ones_high = high_flag.astype(_bf16)
    for kb in range(ATT_KBLKS):
        if kb == 0:
            k_blk, v_blk = kp_ref[...], vp_ref[...]
        elif kb == ATT_KBLKS - 1:
            k_blk, v_blk = kn_ref[...], vn_ref[...]
        else:
            rows = slice((kb - 1) * BLK, kb * BLK)
            k_blk, v_blk = kc_ref[rows, :], vc_ref[rows, :]
        for g in range(N_KV_HEADS):
            tile = slice((g // HEADS_PER_TILE) * LANES, (g // HEADS_PER_TILE + 1) * LANES)
            own = low_lanes if g % HEADS_PER_TILE == 0 else ~low_lanes
            k_own = jnp.where(own, k_blk[:, tile], zero)
            v_own = jnp.where(own, v_blk[:, tile], zero)
            k_swap = pltpu.roll(k_own, HEAD_DIM, 1)
            v_swap = pltpu.roll(v_own, HEAD_DIM, 1)
            if g % HEADS_PER_TILE == 0:
                k_lo, k_hi, v_lo, v_hi = k_own, k_swap, v_own, v_swap
            else:
                k_lo, k_hi, v_lo, v_hi = k_swap, k_own, v_swap, v_own
            base = kb * PAIR_ROWS
            kz_ref[g, base:base + BLK, :] = k_lo
            kz_ref[g, base + BLK:base + PAIR_ROWS, :] = k_hi
            vz_ref[g, base:base + BLK, :LANES] = v_lo
            vz_ref[g, base + BLK:base + PAIR_ROWS, :LANES] = v_hi
            vz_ref[g, base:base + BLK, LANES:] = ones_low
            vz_ref[g, base + BLK:base + PAIR_ROWS, LANES:] = ones_high


def _attn_group_unit(j, g, prev_ok, next_ok, low_lanes_f32, sink_ref, q_ref, kz_ref, vz_ref, o_ref,
                     p_ref, sink_term_ref):
    slot = (j * N_KV_HEADS + g) % ATT_P_SLOTS
    rows = slice(j * BLK, (j + 1) * BLK)
    stack = slice(j * PAIR_ROWS, (j + 3) * PAIR_ROWS)
    q_tiles = [g * PAIRS_PER_GROUP + pair for pair in range(PAIRS_PER_GROUP)]

    def scores(zero_row, row_lo, row_hi):
        del row_hi
        q2 = jnp.concatenate([q_ref[rows, qt * LANES:(qt + 1) * LANES] for qt in q_tiles], axis=0)
        q2 = q2 + zero_row(row_lo).astype(_bf16)
        s_all = lax.dot_general(q2, kz_ref[g, stack, :], (((1,), (1,)), ((), ())),
                                preferred_element_type=_f32)
        for pair, qt in enumerate(q_tiles):
            s = s_all[pair * BLK:(pair + 1) * BLK, :]
            probs = [None] * (3 * HEADS_PER_TILE)
            sink_terms = []
            for half in range(HEADS_PER_TILE):
                sink = sink_ref[qt * HEADS_PER_TILE + half]
                col = lambda kb: slice((kb * HEADS_PER_TILE + half) * BLK, (kb * HEADS_PER_TILE + half + 1) * BLK)
                s_prev = jnp.where(prev_ok, s[:, col(0)], NEG)
                s_own = s[:, col(1)]
                s_next = jnp.where(next_ok, s[:, col(2)], NEG)
                mx = jnp.max(jnp.maximum(jnp.maximum(s_prev, s_own), s_next), axis=-1, keepdims=True)
                mx = jnp.maximum(mx, sink)
                probs[0 * HEADS_PER_TILE + half] = jnp.exp(s_prev - mx).astype(_bf16)
                probs[1 * HEADS_PER_TILE + half] = jnp.exp(s_own - mx).astype(_bf16)
                probs[2 * HEADS_PER_TILE + half] = jnp.exp(s_next - mx).astype(_bf16)
                sink_terms.append(jnp.exp(sink - mx))
            p_ref[slot, pair * BLK:(pair + 1) * BLK, :] = jnp.concatenate(probs, axis=1)
            sink_term_ref[slot, pair * BLK:(pair + 1) * BLK, :] = jnp.where(low_lanes_f32, sink_terms[0],
                                                                            sink_terms[1])

    def values(zero_row, row_lo, row_hi):
        del zero_row, row_lo, row_hi
        o_ext = jnp.dot(p_ref[slot], vz_ref[g, stack, :], preferred_element_type=_f32)
        o_all = o_ext[:, :LANES] / (o_ext[:, LANES:] + sink_term_ref[slot])
        for pair, qt in enumerate(q_tiles):
            o_ref[rows, qt * LANES:(qt + 1) * LANES] = o_all[pair * BLK:(pair + 1) * BLK, :].astype(_bf16)

    return scores, values


def _attn_ffn_kernel(sink_ref, q_ref, kp_ref, kc_ref, kn_ref, vp_ref, vc_ref, vn_ref, xb_ref,
                     wo_ref, g_ref, wgu_ref, wd_ref, gc_ref, w1_ref, b1_ref, out_ref, u_out_ref,
                     kz_ref, vz_ref, o_ref, p_ref, sink_term_ref, act_ref, *, n_tiles, tiles_per_seq):
    s = pl.program_id(0)

    def front_units():
        _stage_kv_stacks(kp_ref, kc_ref, kn_ref, vp_ref, vc_ref, vn_ref, kz_ref, vz_ref)
        has_prev = (s % tiles_per_seq) != 0
        has_next = (s % tiles_per_seq) != tiles_per_seq - 1
        qi = lax.broadcasted_iota(jnp.int32, (BLK, BLK), 0)
        ci = lax.broadcasted_iota(jnp.int32, (BLK, BLK), 1)
        prev_in_window = ci >= qi
        next_in_window = ci <= qi
        low_lanes_f32 = (lax.broadcasted_iota(jnp.int32, (BLK, LANES), 1) & HEAD_DIM) == 0
        halves = []
        for j in range(ATT_QBLKS):
            prev_ok = prev_in_window & has_prev if j == 0 else prev_in_window
            next_ok = next_in_window & has_next if j == ATT_QBLKS - 1 else next_in_window
            for g in range(N_KV_HEADS):
                halves.append(_attn_group_unit(j, g, prev_ok, next_ok, low_lanes_f32, sink_ref, q_ref,
                                               kz_ref, vz_ref, o_ref, p_ref, sink_term_ref))
        units = []
        for i in range(len(halves) + ATT_VALUE_LAG):
            if i < len(halves):
                units.append(halves[i][0])
            if i >= ATT_VALUE_LAG:
                units.append(halves[i - ATT_VALUE_LAG][1])
        return units

    def back_half(units):
        x1 = xb_ref[...] + jnp.dot(o_ref[...], wo_ref[...], preferred_element_type=_f32)
        units = units() if callable(units) else units
        x2 = _ffn_residual(x1, g_ref[...], wgu_ref, wd_ref, act_ref, units, fillers_have_matmuls=True)
        out_ref[...] = x2
        h = _rmsnorm(x2, gc_ref[...]).astype(_bf16)
        for c in range(CONV_CH // FF_CHUNK):
            lo = c * FF_CHUNK
            val = (jnp.dot(h, w1_ref[:, lo:lo + FF_CHUNK], preferred_element_type=_f32)
                   + b1_ref[:, lo:lo + FF_CHUNK])
            gate = (jnp.dot(h, w1_ref[:, CONV_CH + lo:CONV_CH + lo + FF_CHUNK], preferred_element_type=_f32)
                    + b1_ref[:, CONV_CH + lo:CONV_CH + lo + FF_CHUNK])
            u_out_ref[:, lo:lo + FF_CHUNK] = val * jax.nn.sigmoid(gate)

    @pl.when(s == 0)
    def _():
        _run_ungated(front_units())

    @pl.when((s > 0) & (s < n_tiles))
    def _():
        back_half(front_units)

    @pl.when(s == n_tiles)
    def _():
        back_half([])


def _attn_ffn_call(x2, q, k, v, sink, w_o, g, w_gu, w_d, gc, w1, b1, seq):
    m = x2.shape[0]
    n_tiles = m // TOK_TILE
    tiles_per_seq = seq // TOK_TILE
    n_blk = m // BLK
    front = lambda s: jnp.minimum(s, n_tiles - 1)
    back = lambda s: jnp.maximum(s - 1, 0)
    tok_front = lambda s: (front(s), 0)
    tok_back = lambda s: (back(s), 0)
    prev = lambda s: (jnp.maximum(front(s) * ATT_QBLKS - 1, 0), 0)
    nxt = lambda s: (jnp.minimum((front(s) + 1) * ATT_QBLKS, n_blk - 1), 0)
    edge_spec = lambda imap: pl.BlockSpec((BLK, KV_DIM), imap)
    own_spec = pl.BlockSpec((TOK_TILE, KV_DIM), tok_front)
    layer0 = lambda shape: _const_spec((None,) + shape, (0,) * (len(shape) + 1))
    return pl.pallas_call(
        functools.partial(_attn_ffn_kernel, n_tiles=n_tiles, tiles_per_seq=tiles_per_seq),
        grid=(n_tiles + 1,),
        in_specs=[
            pl.BlockSpec(memory_space=pltpu.SMEM),
            pl.BlockSpec((TOK_TILE, Q_DIM), tok_front),
            edge_spec(prev), own_spec, edge_spec(nxt),
            edge_spec(prev), own_spec, edge_spec(nxt),
            pl.BlockSpec((TOK_TILE, D_MODEL), tok_back),
            _const_spec((Q_DIM, D_MODEL)),
            _const_spec((1, D_MODEL)),
            layer0((D_MODEL, 2 * D_FF)),
            layer0((D_FF, D_MODEL)),
            _const_spec((1, D_MODEL)),
            _const_spec((D_MODEL, 2 * CONV_CH)),
            _const_spec((1, 2 * CONV_CH)),
        ],
        out_specs=[
            pl.BlockSpec((TOK_TILE, D_MODEL), tok_back),
            pl.BlockSpec((TOK_TILE, CONV_CH), tok_back),
        ],
        out_shape=[
            jax.ShapeDtypeStruct((m, D_MODEL), _f32),
            jax.ShapeDtypeStruct((m, CONV_CH), _f32),
        ],
        scratch_shapes=[
            pltpu.VMEM((N_KV_HEADS, ATT_KBLKS * PAIR_ROWS, LANES), _bf16),
            pltpu.VMEM((N_KV_HEADS, ATT_KBLKS * PAIR_ROWS, 2 * LANES), _bf16),
            pltpu.VMEM((TOK_TILE, Q_DIM), _bf16),
            pltpu.VMEM((ATT_P_SLOTS, PAIRS_PER_GROUP * BLK, 3 * PAIR_ROWS), _bf16),
            pltpu.VMEM((ATT_P_SLOTS, PAIRS_PER_GROUP * BLK, LANES), _f32),
            pltpu.VMEM((TOK_TILE, D_FF), _bf16),
        ],
        compiler_params=_compiler_params(1),
        name="attn_ffn",
    )(sink, q, k, k, k, v, v, v, x2, w_o, g, w_gu, w_d, gc, w1, b1)


def _conv_ffn_kernel(ua_ref, up_ref, un_ref, xb_ref, wdw_ref, bdw_ref, lng_ref, lnb_ref,
                     w2_ref, b2_ref, gf_ref, wgu_ref, wd_ref, gfin_ref, out_ref,
                     u_ref, c_ref, y_ref, act_ref, *, n_tiles, tiles_per_seq):
    s = pl.program_id(0)

    @pl.when(s == 0)
    def _():
        y_ref[...] = jnp.zeros_like(y_ref)

    t_idx = jnp.minimum(s, n_tiles - 1)
    has_prev = (t_idx % tiles_per_seq) != 0
    has_next = (t_idx % tiles_per_seq) != tiles_per_seq - 1
    u_prev = jnp.where(has_prev, up_ref[...], 0.0)
    u_next = jnp.where(has_next, un_ref[...], 0.0)
    for t in range(N_LANE_TILES):
        tile = slice(t * LANES, (t + 1) * LANES)
        u_ref[t, :HALO, :] = u_prev[:, tile]
        u_ref[t, HALO:HALO + TOK_TILE, :] = ua_ref[:, tile]
        u_ref[t, HALO + TOK_TILE:, :] = u_next[:, tile]

    def conv_unit(t, rc):
        def run(zero_row, row_lo, row_hi):
            acc = None
            for j in range(CONV_WIDTH):
                start = rc * CONV_ROWS + HALO - CONV_PAD + j
                w_rows = wdw_ref[t, j] + zero_row(row_lo + (j * (row_hi - row_lo)) // CONV_WIDTH)
                taps = u_ref[t, pl.ds(start, CONV_ROWS), :].reshape(CONV_ROWS // SUBLANES, SUBLANES, LANES)
                prod = taps * w_rows[None]
                acc = prod if acc is None else acc + prod
            c_ref[t, pl.ds(rc * CONV_ROWS, CONV_ROWS), :] = acc.reshape(CONV_ROWS, LANES) + bdw_ref[t]
        return run

    def norm_unit(rc):
        def run(zero_row, row_lo, row_hi):
            del zero_row, row_lo, row_hi
            rsl = pl.ds(rc * CONV_ROWS, CONV_ROWS)
            c = jnp.concatenate([c_ref[t, rsl, :] for t in range(N_LANE_TILES)], axis=1)
            mu = jnp.mean(c, axis=-1, keepdims=True)
            var = jnp.mean(jnp.square(c - mu), axis=-1, keepdims=True)
            y = (c - mu) * lax.rsqrt(var + EPS) * lng_ref[...] + lnb_ref[...]
            y_ref[rsl, :] = (y * jax.nn.sigmoid(y)).astype(_bf16)
        return run

    n_rc = TOK_TILE // CONV_ROWS
    units = [conv_unit(t, rc) for t in range(N_LANE_TILES) for rc in range(n_rc)]
    units += [norm_unit(rc) for rc in range(n_rc)]

    n_pw2 = D_MODEL // FF_CHUNK
    n_ffn = D_FF // FF_CHUNK + D_MODEL // FF_CHUNK
    n_early = (len(units) * n_pw2) // (n_pw2 + n_ffn)
    pw2_groups = _spread(units[:n_early], n_pw2)
    x1_cols = []
    for c in range(n_pw2):
        lo = c * FF_CHUNK
        x1_c = (xb_ref[:, lo:lo + FF_CHUNK]
                + jnp.dot(y_ref[...], w2_ref[:, lo:lo + FF_CHUNK], preferred_element_type=_f32)
                + b2_ref[:, lo:lo + FF_CHUNK])
        x1_cols.append(x1_c)
        _run_fillers(pw2_groups[c], x1_c)
    x1 = jnp.concatenate(x1_cols, axis=1)
    x2 = _ffn_residual(x1, gf_ref[...], wgu_ref, wd_ref, act_ref, units[n_early:], fillers_have_matmuls=False)
    out_ref[...] = _rmsnorm(x2, gfin_ref[...])


def _conv_ffn_call(x2, u, wdw, bdw, lng, lnb, w2, b2, gf, w_gu, w_d, gfin, seq):
    m = x2.shape[0]
    n_tiles = m // TOK_TILE
    tiles_per_seq = seq // TOK_TILE
    halo_per_tile = TOK_TILE // HALO
    n_halo = m // HALO
    front = lambda s: jnp.minimum(s, n_tiles - 1)
    back = lambda s: jnp.maximum(s - 1, 0)
    tok_front = lambda s: (front(s), 0)
    tok_back = lambda s: (back(s), 0)
    prev = lambda s: (jnp.maximum(front(s) * halo_per_tile - 1, 0), 0)
    nxt = lambda s: (jnp.minimum((front(s) + 1) * halo_per_tile, n_halo - 1), 0)
    rows = TOK_TILE + 2 * HALO
    layer1 = lambda shape: _const_spec((None,) + shape, (1,) + (0,) * len(shape))
    return pl.pallas_call(
        functools.partial(_conv_ffn_kernel, n_tiles=n_tiles, tiles_per_seq=tiles_per_seq),
        grid=(n_tiles + 1,),
        in_specs=[
            pl.BlockSpec((TOK_TILE, CONV_CH), tok_front),
            pl.BlockSpec((HALO, CONV_CH), prev),
            pl.BlockSpec((HALO, CONV_CH), nxt),
            pl.BlockSpec((TOK_TILE, D_MODEL), tok_back),
            _const_spec((N_LANE_TILES, CONV_WIDTH, SUBLANES, LANES)),
            _const_spec((N_LANE_TILES, 1, LANES)),
            _const_spec((1, CONV_CH)),
            _const_spec((1, CONV_CH)),
            _const_spec((CONV_CH, D_MODEL)),
            _const_spec((1, D_MODEL)),
            _const_spec((1, D_MODEL)),
            layer1((D_MODEL, 2 * D_FF)),
            layer1((D_FF, D_MODEL)),
            _const_spec((1, D_MODEL)),
        ],
        out_specs=pl.BlockSpec((TOK_TILE, D_MODEL), tok_back),
        out_shape=jax.ShapeDtypeStruct((m, D_MODEL), _f32),
        scratch_shapes=[
            pltpu.VMEM((N_LANE_TILES, rows, LANES), _f32),
            pltpu.VMEM((N_LANE_TILES, TOK_TILE, LANES), _f32),
            pltpu.VMEM((TOK_TILE, CONV_CH), _bf16),
            pltpu.VMEM((TOK_TILE, D_FF), _bf16),
        ],
        compiler_params=_compiler_params(1),
        name="conv_ffn",
    )(u, u, u, x2, wdw, bdw, lng, lnb, w2, b2, gf, w_gu, w_d, gfin)


def _rope_tables(seq):
    pos = np.arange(seq, dtype=np.float64)
    inv_freq = ROPE_THETA ** (-np.arange(0, ROT_DIM, 2, dtype=np.float64) / ROT_DIM)
    ang = pos[:, None] * inv_freq[None, :]
    cos = np.cos(ang)
    sin = np.sin(ang)
    ones = np.ones((seq, HEAD_DIM - ROT_DIM))
    zeros_half = np.zeros((seq, ROT_HALF))
    zeros_rest = np.zeros((seq, HEAD_DIM - ROT_DIM))
    cos_h = np.concatenate([cos, cos, ones], axis=1)
    sin_hi_h = np.concatenate([zeros_half, sin, zeros_rest], axis=1)
    sin_lo_h = np.concatenate([-sin, zeros_half, zeros_rest], axis=1)
    rep = lambda a: jnp.asarray(np.tile(a, (1, HEADS_PER_TILE)), dtype=_f32)
    return rep(cos_h), rep(sin_hi_h), rep(sin_lo_h)


def kernel(x, attn_norm, attn_w_qkv, attn_w_o, attn_sink, conv_norm, conv_w_pw1, conv_b_pw1, conv_w_dw,
           conv_b_dw, conv_ln_g, conv_ln_b, conv_w_pw2, conv_b_pw2, ffn_norm, ffn_w_gu, ffn_w_down, final_norm):
    b, s, d = x.shape
    assert d == D_MODEL and s % TOK_TILE == 0
    assert attn_norm.shape[0] == 1 and conv_norm.shape[0] == 1 and ffn_norm.shape[0] == 2
    m = b * s
    x2 = x.reshape(m, d)
    row = lambda v: v.reshape(1, -1).astype(_f32)
    bf = lambda w: w.astype(_bf16)
    w_gu = bf(ffn_w_gu)
    w_d = bf(ffn_w_down)

    cos_t, sin_hi_t, sin_lo_t = _rope_tables(s)
    q, k, v = _qkv_call(x2, row(attn_norm[0]), bf(attn_w_qkv[0]), cos_t, sin_hi_t, sin_lo_t, s)
    x2, u = _attn_ffn_call(x2, q, k, v, attn_sink[0].astype(_f32), bf(attn_w_o[0]), row(ffn_norm[0]),
                           w_gu, w_d, row(conv_norm[0]), bf(conv_w_pw1[0]), row(conv_b_pw1[0]), s)

    wdw = conv_w_dw[0].reshape(CONV_WIDTH, N_LANE_TILES, LANES).transpose(1, 0, 2)
    wdw = jnp.broadcast_to(wdw[:, :, None, :], (N_LANE_TILES, CONV_WIDTH, SUBLANES, LANES))
    bdw = conv_b_dw[0].reshape(N_LANE_TILES, 1, LANES)
    out = _conv_ffn_call(x2, u, wdw, bdw, row(conv_ln_g[0]), row(conv_ln_b[0]), bf(conv_w_pw2[0]),
                         row(conv_b_pw2[0]), row(ffn_norm[1]), w_gu, w_d, row(final_norm), s)
    return out.reshape(b, s, d)
```

```python
import functools
import math

import jax
import jax.numpy as jnp
import numpy as np
from jax import lax
from jax.experimental import pallas as pl
from jax.experimental.pallas import tpu as pltpu

D_MODEL = 1024
N_HEADS = 16
N_KV_HEADS = 4
HEAD_DIM = D_MODEL // N_HEADS
GROUP = N_HEADS // N_KV_HEADS
ROT_DIM = HEAD_DIM // 4
ROT_HALF = ROT_DIM // 2
ROPE_THETA = 500000.0
WINDOW = 128
BLK = 128
Q_DIM = N_HEADS * HEAD_DIM
KV_DIM = N_KV_HEADS * HEAD_DIM
QKV_DIM = Q_DIM + 2 * KV_DIM
CONV_CH = D_MODEL
CONV_WIDTH = 31
CONV_PAD = (CONV_WIDTH - 1) // 2
D_FF = ((8 * D_MODEL // 3 + 255) // 256) * 256
EPS = 1e-6
NEG = -1e30

LANES = 128
SUBLANES = 8
VMEM_LIMIT_BYTES = 56 * 1024 * 1024

TOK_TILE = 512
QKV_TILE = 1024
FF_CHUNK = 256
HALO = 2 * SUBLANES
CONV_ROWS = 64
N_LANE_TILES = D_MODEL // LANES
HEADS_PER_TILE = LANES // HEAD_DIM
PAIRS_PER_GROUP = GROUP // HEADS_PER_TILE
ATT_QBLKS = TOK_TILE // BLK
ATT_KBLKS = ATT_QBLKS + 2
ATT_VALUE_LAG = 2
ATT_P_SLOTS = 2 * (ATT_VALUE_LAG + 1)
PAIR_ROWS = HEADS_PER_TILE * BLK

_f32 = jnp.float32
_bf16 = jnp.bfloat16


def _rmsnorm(xf, g):
    y = xf * lax.rsqrt(jnp.mean(xf * xf, axis=-1, keepdims=True) + EPS)
    return y * g


def _const_spec(shape, index=None):
    index = (0,) * len(shape) if index is None else index
    return pl.BlockSpec(shape, lambda *_: index, pipeline_mode=pl.Buffered(1))


def _compiler_params(n_axes):
    return pltpu.CompilerParams(
        dimension_semantics=("arbitrary",) * n_axes,
        vmem_limit_bytes=VMEM_LIMIT_BYTES,
    )


def _qkv_kernel(x_ref, g_ref, w_ref, cos_ref, sin_hi_ref, sin_lo_ref, q_ref, k_ref, v_ref):
    h = _rmsnorm(x_ref[...], g_ref[...]).astype(_bf16)
    qkv = jnp.dot(h, w_ref[...], preferred_element_type=_f32)
    cos = cos_ref[...]
    sin_hi = sin_hi_ref[...]
    sin_lo = sin_lo_ref[...]
    scale = 1.0 / math.sqrt(HEAD_DIM)

    def rope(t):
        return (t * cos + pltpu.roll(t, ROT_HALF, 1) * sin_hi
                + pltpu.roll(t, LANES - ROT_HALF, 1) * sin_lo)

    for j in range(Q_DIM // LANES):
        t = qkv[:, j * LANES:(j + 1) * LANES]
        q_ref[:, j * LANES:(j + 1) * LANES] = (rope(t) * scale).astype(_bf16)
    for j in range(KV_DIM // LANES):
        t = qkv[:, Q_DIM + j * LANES:Q_DIM + (j + 1) * LANES]
        k_ref[:, j * LANES:(j + 1) * LANES] = rope(t).astype(_bf16)
    v_ref[...] = qkv[:, Q_DIM + KV_DIM:].astype(_bf16)


def _qkv_call(x2, g, w_qkv, cos_t, sin_hi_t, sin_lo_t, seq):
    m = x2.shape[0]
    tiles_per_seq = seq // QKV_TILE
    tok = lambda i: (i, 0)
    pos = lambda i: (i % tiles_per_seq, 0)
    return pl.pallas_call(
        _qkv_kernel,
        grid=(m // QKV_TILE,),
        in_specs=[
            pl.BlockSpec((QKV_TILE, D_MODEL), tok),
            _const_spec((1, D_MODEL)),
            _const_spec((D_MODEL, QKV_DIM)),
            pl.BlockSpec((QKV_TILE, LANES), pos),
            pl.BlockSpec((QKV_TILE, LANES), pos),
            pl.BlockSpec((QKV_TILE, LANES), pos),
        ],
        out_specs=[
            pl.BlockSpec((QKV_TILE, Q_DIM), tok),
            pl.BlockSpec((QKV_TILE, KV_DIM), tok),
            pl.BlockSpec((QKV_TILE, KV_DIM), tok),
        ],
        out_shape=[
            jax.ShapeDtypeStruct((m, Q_DIM), _bf16),
            jax.ShapeDtypeStruct((m, KV_DIM), _bf16),
            jax.ShapeDtypeStruct((m, KV_DIM), _bf16),
        ],
        compiler_params=_compiler_params(1),
        name="qkv_rope",
    )(x2, g, w_qkv, cos_t, sin_hi_t, sin_lo_t)


def _spread(thunks, n_slots):
    bounds = [(len(thunks) * i) // n_slots for i in range(n_slots + 1)]
    return [thunks[bounds[i]:bounds[i + 1]] for i in range(n_slots)]


def _run_fillers(group, produced):
    cache = {}

    def zero_row(r):
        r = r // SUBLANES * SUBLANES
        if r not in cache:
            cache[r] = jnp.minimum(jnp.abs(produced[r:r + SUBLANES, :LANES]), 0.0)
        return cache[r][:1, :]

    n_rows = produced.shape[0]
    for i, thunk in enumerate(group):
        thunk(zero_row, (i * n_rows) // len(group), ((i + 1) * n_rows) // len(group))


def _run_ungated(group):
    zero = jnp.zeros((1, LANES), _f32)
    for thunk in group:
        thunk(lambda r: zero, 0, 0)


def _ffn_residual(xf, g, wgu_ref, wd_ref, act_ref, fillers, fillers_have_matmuls):
    n_up = D_FF // FF_CHUNK
    n_down = D_MODEL // FF_CHUNK
    groups = _spread(list(fillers), n_up + n_down)
    h = _rmsnorm(xf, g).astype(_bf16)
    previous = xf
    for c in range(n_up):
        lo = c * FF_CHUNK
        gate = jnp.dot(h, wgu_ref[:, lo:lo + FF_CHUNK], preferred_element_type=_f32)
        up = jnp.dot(h, wgu_ref[:, D_FF + lo:D_FF + lo + FF_CHUNK], preferred_element_type=_f32)
        act = gate * jax.nn.sigmoid(gate) * up
        act_ref[:, lo:lo + FF_CHUNK] = act.astype(_bf16)
        _run_fillers(groups[c], previous if fillers_have_matmuls else act)
        previous = act
    outs = []
    for c in range(n_down):
        lo = c * FF_CHUNK
        out = xf[:, lo:lo + FF_CHUNK] + jnp.dot(act_ref[...], wd_ref[:, lo:lo + FF_CHUNK],
                                                 preferred_element_type=_f32)
        outs.append(out)
        _run_fillers(groups[n_up + c], previous if fillers_have_matmuls else out)
        previous = out
    return jnp.concatenate(outs, axis=1)


def _stage_kv_stacks(kp_ref, kc_ref, kn_ref, vp_ref, vc_ref, vn_ref, kz_ref, vz_ref):
    lane = lax.broadcasted_iota(jnp.int32, (BLK, LANES), 1)
    low_lanes = lane < HEAD_DIM
    zero = jnp.zeros((), _bf16)
    high_flag = (lane // HEAD_DIM).astype(_f32)
    ones_low = (1.0 - high_flag).astype(_bf16)
    ones_high = high_flag.astype(_bf16)
    for kb in range(ATT_KBLKS):
        if kb == 0:
            k_blk, v_blk = kp_ref[...], vp_ref[...]
        elif kb == ATT_KBLKS - 1:
            k_blk, v_blk = kn_ref[...], vn_ref[...]
        else:
            rows = slice((kb - 1) * BLK, kb * BLK)
            k_blk, v_blk = kc_ref[rows, :], vc_ref[rows, :]
        for g in range(N_KV_HEADS):
            tile = slice((g // HEADS_PER_TILE) * LANES, (g // HEADS_PER_TILE + 1) * LANES)
            own = low_lanes if g % HEADS_PER_TILE == 0 else ~low_lanes
            k_own = jnp.where(own, k_blk[:, tile], zero)
            v_own = jnp.where(own, v_blk[:, tile], zero)
            k_swap = pltpu.roll(k_own, HEAD_DIM, 1)
            v_swap = pltpu.roll(v_own, HEAD_DIM, 1)
            if g % HEADS_PER_TILE == 0:
                k_lo, k_hi, v_lo, v_hi = k_own, k_swap, v_own, v_swap
            else:
                k_lo, k_hi, v_lo, v_hi = k_swap, k_own, v_swap, v_own
            base = kb * PAIR_ROWS
            kz_ref[g, base:base + BLK, :] = k_lo
            kz_ref[g, base + BLK:base + PAIR_ROWS, :] = k_hi
            vz_ref[g, base:base + BLK, :LANES] = v_lo
            vz_ref[g, base + BLK:base + PAIR_ROWS, :LANES] = v_hi
            vz_ref[g, base:base + BLK, LANES:] = ones_low
            vz_ref[g, base + BLK:base + PAIR_ROWS, LANES:] = ones_high


def _attn_group_unit(j, g, prev_ok, next_ok, low_lanes_f32, sink_ref, q_ref, kz_ref, vz_ref, o_ref,
                     p_ref, sink_term_ref):
    slot = (j * N_KV_HEADS + g) % ATT_P_SLOTS
    rows = slice(j * BLK, (j + 1) * BLK)
    stack = slice(j * PAIR_ROWS, (j + 3) * PAIR_ROWS)
    q_tiles = [g * PAIRS_PER_GROUP + pair for pair in range(PAIRS_PER_GROUP)]

    def scores(zero_row, row_lo, row_hi):
        del row_hi
        q2 = jnp.concatenate([q_ref[rows, qt * LANES:(qt + 1) * LANES] for qt in q_tiles], axis=0)
        q2 = q2 + zero_row(row_lo).astype(_bf16)
        s_all = lax.dot_general(q2, kz_ref[g, stack, :], (((1,), (1,)), ((), ())),
                                preferred_element_type=_f32)
        for pair, qt in enumerate(q_tiles):
            s = s_all[pair * BLK:(pair + 1) * BLK, :]
            probs = [None] * (3 * HEADS_PER_TILE)
            sink_terms = []
            for half in range(HEADS_PER_TILE):
                sink = sink_ref[qt * HEADS_PER_TILE + half]
                col = lambda kb: slice((kb * HEADS_PER_TILE + half) * BLK, (kb * HEADS_PER_TILE + half + 1) * BLK)
                s_prev = jnp.where(prev_ok, s[:, col(0)], NEG)
                s_own = s[:, col(1)]
                s_next = jnp.where(next_ok, s[:, col(2)], NEG)
                mx = jnp.max(jnp.maximum(jnp.maximum(s_prev, s_own), s_next), axis=-1, keepdims=True)
                mx = jnp.maximum(mx, sink)
                probs[0 * HEADS_PER_TILE + half] = jnp.exp(s_prev - mx).astype(_bf16)
                probs[1 * HEADS_PER_TILE + half] = jnp.exp(s_own - mx).astype(_bf16)
                probs[2 * HEADS_PER_TILE + half] = jnp.exp(s_next - mx).astype(_bf16)
                sink_terms.append(jnp.exp(sink - mx))
            p_ref[slot, pair * BLK:(pair + 1) * BLK, :] = jnp.concatenate(probs, axis=1)
            sink_term_ref[slot, pair * BLK:(pair + 1) * BLK, :] = jnp.where(low_lanes_f32, sink_terms[0],
                                                                            sink_terms[1])

    def values(zero_row, row_lo, row_hi):
        del zero_row, row_lo, row_hi
        o_ext = jnp.dot(p_ref[slot], vz_ref[g, stack, :], preferred_element_type=_f32)
        o_all = o_ext[:, :LANES] / (o_ext[:, LANES:] + sink_term_ref[slot])
        for pair, qt in enumerate(q_tiles):
            o_ref[rows, qt * LANES:(qt + 1) * LANES] = o_all[pair * BLK:(pair + 1) * BLK, :].astype(_bf16)

    return scores, values


def _attn_ffn_kernel(sink_ref, q_ref, kp_ref, kc_ref, kn_ref, vp_ref, vc_ref, vn_ref, xb_ref,
                     wo_ref, g_ref, wgu_ref, wd_ref, gc_ref, w1_ref, b1_ref, out_ref, u_out_ref,
                     kz_ref, vz_ref, o_ref, p_ref, sink_term_ref, act_ref, *, n_tiles, tiles_per_seq):
    s = pl.program_id(0)

    def front_units():
        _stage_kv_stacks(kp_ref, kc_ref, kn_ref, vp_ref, vc_ref, vn_ref, kz_ref, vz_ref)
        has_prev = (s % tiles_per_seq) != 0
        has_next = (s % tiles_per_seq) != tiles_per_seq - 1
        qi = lax.broadcasted_iota(jnp.int32, (BLK, BLK), 0)
        ci = lax.broadcasted_iota(jnp.int32, (BLK, BLK), 1)
        prev_in_window = ci >= qi
        next_in_window = ci <= qi
        low_lanes_f32 = (lax.broadcasted_iota(jnp.int32, (BLK, LANES), 1) & HEAD_DIM) == 0
        halves = []
        for j in range(ATT_QBLKS):
            prev_ok = prev_in_window & has_prev if j == 0 else prev_in_window
            next_ok = next_in_window & has_next if j == ATT_QBLKS - 1 else next_in_window
            for g in range(N_KV_HEADS):
                halves.append(_attn_group_unit(j, g, prev_ok, next_ok, low_lanes_f32, sink_ref, q_ref,
                                               kz_ref, vz_ref, o_ref, p_ref, sink_term_ref))
        units = []
        for i in range(len(halves) + ATT_VALUE_LAG):
            if i < len(halves):
                units.append(halves[i][0])
            if i >= ATT_VALUE_LAG:
                units.append(halves[i - ATT_VALUE_LAG][1])
        return units

    def back_half(units):
        x1 = xb_ref[...] + jnp.dot(o_ref[...], wo_ref[...], preferred_element_type=_f32)
        units = units() if callable(units) else units
        x2 = _ffn_residual(x1, g_ref[...], wgu_ref, wd_ref, act_ref, units, fillers_have_matmuls=True)
        out_ref[...] = x2
        h = _rmsnorm(x2, gc_ref[...]).astype(_bf16)
        for c in range(CONV_CH // FF_CHUNK):
            lo = c * FF_CHUNK
            val = (jnp.dot(h, w1_ref[:, lo:lo + FF_CHUNK], preferred_element_type=_f32)
                   + b1_ref[:, lo:lo + FF_CHUNK])
            gate = (jnp.dot(h, w1_ref[:, CONV_CH + lo:CONV_CH + lo + FF_CHUNK], preferred_element_type=_f32)
                    + b1_ref[:, CONV_CH + lo:CONV_CH + lo + FF_CHUNK])
            u_out_ref[:, lo:lo + FF_CHUNK] = val * jax.nn.sigmoid(gate)

    @pl.when(s == 0)
    def _():
        _run_ungated(front_units())

    @pl.when((s > 0) & (s < n_tiles))
    def _():
        back_half(front_units)

    @pl.when(s == n_tiles)
    def _():
        back_half([])


def _attn_ffn_call(x2, q, k, v, sink, w_o, g, w_gu, w_d, gc, w1, b1, seq):
    m = x2.shape[0]
    n_tiles = m // TOK_TILE
    tiles_per_seq = seq // TOK_TILE
    n_blk = m // BLK
    front = lambda s: jnp.minimum(s, n_tiles - 1)
    back = lambda s: jnp.maximum(s - 1, 0)
    tok_front = lambda s: (front(s), 0)
    tok_back = lambda s: (back(s), 0)
    prev = lambda s: (jnp.maximum(front(s) * ATT_QBLKS - 1, 0), 0)
    nxt = lambda s: (jnp.minimum((front(s) + 1) * ATT_QBLKS, n_blk - 1), 0)
    edge_spec = lambda imap: pl.BlockSpec((BLK, KV_DIM), imap)
    own_spec = pl.BlockSpec((TOK_TILE, KV_DIM), tok_front)
    layer0 = lambda shape: _const_spec((None,) + shape, (0,) * (len(shape) + 1))
    return pl.pallas_call(
        functools.partial(_attn_ffn_kernel, n_tiles=n_tiles, tiles_per_seq=tiles_per_seq),
        grid=(n_tiles + 1,),
        in_specs=[
            pl.BlockSpec(memory_space=pltpu.SMEM),
            pl.BlockSpec((TOK_TILE, Q_DIM), tok_front),
            edge_spec(prev), own_spec, edge_spec(nxt),
            edge_spec(prev), own_spec, edge_spec(nxt),
            pl.BlockSpec((TOK_TILE, D_MODEL), tok_back),
            _const_spec((Q_DIM, D_MODEL)),
            _const_spec((1, D_MODEL)),
            layer0((D_MODEL, 2 * D_FF)),
            layer0((D_FF, D_MODEL)),
            _const_spec((1, D_MODEL)),
            _const_spec((D_MODEL, 2 * CONV_CH)),
            _const_spec((1, 2 * CONV_CH)),
        ],
        out_specs=[
            pl.BlockSpec((TOK_TILE, D_MODEL), tok_back),
            pl.BlockSpec((TOK_TILE, CONV_CH), tok_back),
        ],
        out_shape=[
            jax.ShapeDtypeStruct((m, D_MODEL), _f32),
            jax.ShapeDtypeStruct((m, CONV_CH), _f32),
        ],
        scratch_shapes=[
            pltpu.VMEM((N_KV_HEADS, ATT_KBLKS * PAIR_ROWS, LANES), _bf16),
            pltpu.VMEM((N_KV_HEADS, ATT_KBLKS * PAIR_ROWS, 2 * LANES), _bf16),
            pltpu.VMEM((TOK_TILE, Q_DIM), _bf16),
            pltpu.VMEM((ATT_P_SLOTS, PAIRS_PER_GROUP * BLK, 3 * PAIR_ROWS), _bf16),
            pltpu.VMEM((ATT_P_SLOTS, PAIRS_PER_GROUP * BLK, LANES), _f32),
            pltpu.VMEM((TOK_TILE, D_FF), _bf16),
        ],
        compiler_params=_compiler_params(1),
        name="attn_ffn",
    )(sink, q, k, k, k, v, v, v, x2, w_o, g, w_gu, w_d, gc, w1, b1)


def _conv_ffn_kernel(ua_ref, up_ref, un_ref, xb_ref, wdw_ref, bdw_ref, lng_ref, lnb_ref,
                     w2_ref, b2_ref, gf_ref, wgu_ref, wd_ref, gfin_ref, out_ref,
                     u_ref, c_ref, y_ref, act_ref, *, n_tiles, tiles_per_seq):
    s = pl.program_id(0)

    @pl.when(s == 0)
    def _():
        y_ref[...] = jnp.zeros_like(y_ref)

    t_idx = jnp.minimum(s, n_tiles - 1)
    has_prev = (t_idx % tiles_per_seq) != 0
    has_next = (t_idx % tiles_per_seq) != tiles_per_seq - 1
    u_prev = jnp.where(has_prev, up_ref[...], 0.0)
    u_next = jnp.where(has_next, un_ref[...], 0.0)
    for t in range(N_LANE_TILES):
        tile = slice(t * LANES, (t + 1) * LANES)
        u_ref[t, :HALO, :] = u_prev[:, tile]
        u_ref[t, HALO:HALO + TOK_TILE, :] = ua_ref[:, tile]
        u_ref[t, HALO + TOK_TILE:, :] = u_next[:, tile]

    def conv_unit(t, rc):
        def run(zero_row, row_lo, row_hi):
            acc = None
            for j in range(CONV_WIDTH):
                start = rc * CONV_ROWS + HALO - CONV_PAD + j
                w_rows = wdw_ref[t, j] + zero_row(row_lo + (j * (row_hi - row_lo)) // CONV_WIDTH)
                taps = u_ref[t, pl.ds(start, CONV_ROWS), :].reshape(CONV_ROWS // SUBLANES, SUBLANES, LANES)
                prod = taps * w_rows[None]
                acc = prod if acc is None else acc + prod
            c_ref[t, pl.ds(rc * CONV_ROWS, CONV_ROWS), :] = acc.reshape(CONV_ROWS, LANES) + bdw_ref[t]
        return run

    def norm_unit(rc):
        def run(zero_row, row_lo, row_hi):
            del zero_row, row_lo, row_hi
            rsl = pl.ds(rc * CONV_ROWS, CONV_ROWS)
            c = jnp.concatenate([c_ref[t, rsl, :] for t in range(N_LANE_TILES)], axis=1)
            mu = jnp.mean(c, axis=-1, keepdims=True)
            var = jnp.mean(jnp.square(c - mu), axis=-1, keepdims=True)
            y = (c - mu) * lax.rsqrt(var + EPS) * lng_ref[...] + lnb_ref[...]
            y_ref[rsl, :] = (y * jax.nn.sigmoid(y)).astype(_bf16)
        return run

    n_rc = TOK_TILE // CONV_ROWS
    units = [conv_unit(t, rc) for t in range(N_LANE_TILES) for rc in range(n_rc)]
    units += [norm_unit(rc) for rc in range(n_rc)]

    n_pw2 = D_MODEL // FF_CHUNK
    n_ffn = D_FF // FF_CHUNK + D_MODEL // FF_CHUNK
    n_early = (len(units) * n_pw2) // (n_pw2 + n_ffn)
    pw2_groups = _spread(units[:n_early], n_pw2)
    x1_cols = []
    for c in range(n_pw2):
        lo = c * FF_CHUNK
        x1_c = (xb_ref[:, lo:lo + FF_CHUNK]
                + jnp.dot(y_ref[...], w2_ref[:, lo:lo + FF_CHUNK], preferred_element_type=_f32)
                + b2_ref[:, lo:lo + FF_CHUNK])
        x1_cols.append(x1_c)
        _run_fillers(pw2_groups[c], x1_c)
    x1 = jnp.concatenate(x1_cols, axis=1)
    x2 = _ffn_residual(x1, gf_ref[...], wgu_ref, wd_ref, act_ref, units[n_early:], fillers_have_matmuls=False)
    out_ref[...] = _rmsnorm(x2, gfin_ref[...])


def _conv_ffn_call(x2, u, wdw, bdw, lng, lnb, w2, b2, gf, w_gu, w_d, gfin, seq):
    m = x2.shape[0]
    n_tiles = m // TOK_TILE
    tiles_per_seq = seq // TOK_TILE
    halo_per_tile = TOK_TILE // HALO
    n_halo = m // HALO
    front = lambda s: jnp.minimum(s, n_tiles - 1)
    back = lambda s: jnp.maximum(s - 1, 0)
    tok_front = lambda s: (front(s), 0)
    tok_back = lambda s: (back(s), 0)
    prev = lambda s: (jnp.maximum(front(s) * halo_per_tile - 1, 0), 0)
    nxt = lambda s: (jnp.minimum((front(s) + 1) * halo_per_tile, n_halo - 1), 0)
    rows = TOK_TILE + 2 * HALO
    layer1 = lambda shape: _const_spec((None,) + shape, (1,) + (0,) * len(shape))
    return pl.pallas_call(
        functools.partial(_conv_ffn_kernel, n_tiles=n_tiles, tiles_per_seq=tiles_per_seq),
        grid=(n_tiles + 1,),
        in_specs=[
            pl.BlockSpec((TOK_TILE, CONV_CH), tok_front),
            pl.BlockSpec((HALO, CONV_CH), prev),
            pl.BlockSpec((HALO, CONV_CH), nxt),
            pl.BlockSpec((TOK_TILE, D_MODEL), tok_back),
            _const_spec((N_LANE_TILES, CONV_WIDTH, SUBLANES, LANES)),
            _const_spec((N_LANE_TILES, 1, LANES)),
            _const_spec((1, CONV_CH)),
            _const_spec((1, CONV_CH)),
            _const_spec((CONV_CH, D_MODEL)),
            _const_spec((1, D_MODEL)),
            _const_spec((1, D_MODEL)),
            layer1((D_MODEL, 2 * D_FF)),
            layer1((D_FF, D_MODEL)),
            _const_spec((1, D_MODEL)),
        ],
        out_specs=pl.BlockSpec((TOK_TILE, D_MODEL), tok_back),
        out_shape=jax.ShapeDtypeStruct((m, D_MODEL), _f32),
        scratch_shapes=[
            pltpu.VMEM((N_LANE_TILES, rows, LANES), _f32),
            pltpu.VMEM((N_LANE_TILES, TOK_TILE, LANES), _f32),
            pltpu.VMEM((TOK_TILE, CONV_CH), _bf16),
            pltpu.VMEM((TOK_TILE, D_FF), _bf16),
        ],
        compiler_params=_compiler_params(1),
        name="conv_ffn",
    )(u, u, u, x2, wdw, bdw, lng, lnb, w2, b2, gf, w_gu, w_d, gfin)


def _rope_tables(seq):
    pos = np.arange(seq, dtype=np.float64)
    inv_freq = ROPE_THETA ** (-np.arange(0, ROT_DIM, 2, dtype=np.float64) / ROT_DIM)
    ang = pos[:, None] * inv_freq[None, :]
    cos = np.cos(ang)
    sin = np.sin(ang)
    ones = np.ones((seq, HEAD_DIM - ROT_DIM))
    zeros_half = np.zeros((seq, ROT_HALF))
    zeros_rest = np.zeros((seq, HEAD_DIM - ROT_DIM))
    cos_h = np.concatenate([cos, cos, ones], axis=1)
    sin_hi_h = np.concatenate([zeros_half, sin, zeros_rest], axis=1)
    sin_lo_h = np.concatenate([-sin, zeros_half, zeros_rest], axis=1)
    rep = lambda a: jnp.asarray(np.tile(a, (1, HEADS_PER_TILE)), dtype=_f32)
    return rep(cos_h), rep(sin_hi_h), rep(sin_lo_h)


def kernel(x, attn_norm, attn_w_qkv, attn_w_o, attn_sink, conv_norm, conv_w_pw1, conv_b_pw1, conv_w_dw,
           conv_b_dw, conv_ln_g, conv_ln_b, conv_w_pw2, conv_b_pw2, ffn_norm, ffn_w_gu, ffn_w_down, final_norm):
    b, s, d = x.shape
    assert d == D_MODEL and s % TOK_TILE == 0 and s % QKV_TILE == 0
    assert attn_norm.shape[0] == 1 and conv_norm.shape[0] == 1 and ffn_norm.shape[0] == 2
    m = b * s
    x2 = x.reshape(m, d)
    row = lambda v: v.reshape(1, -1).astype(_f32)
    bf = lambda w: w.astype(_bf16)
    w_gu = bf(ffn_w_gu)
    w_d = bf(ffn_w_down)

    cos_t, sin_hi_t, sin_lo_t = _rope_tables(s)
    q, k, v = _qkv_call(x2, row(attn_norm[0]), bf(attn_w_qkv[0]), cos_t, sin_hi_t, sin_lo_t, s)
    x2, u = _attn_ffn_call(x2, q, k, v, attn_sink[0].astype(_f32), bf(attn_w_o[0]), row(ffn_norm[0]),
                           w_gu, w_d, row(conv_norm[0]), bf(conv_w_pw1[0]), row(conv_b_pw1[0]), s)

    wdw = conv_w_dw[0].reshape(CONV_WIDTH, N_LANE_TILES, LANES).transpose(1, 0, 2)
    wdw = jnp.broadcast_to(wdw[:, :, None, :], (N_LANE_TILES, CONV_WIDTH, SUBLANES, LANES))
    bdw = conv_b_dw[0].reshape(N_LANE_TILES, 1, LANES)
    out = _conv_ffn_call(x2, u, wdw, bdw, row(conv_ln_g[0]), row(conv_ln_b[0]), bf(conv_w_pw2[0]),
                         row(conv_b_pw2[0]), row(ffn_norm[1]), w_gu, w_d, row(final_norm), s)
    return out.reshape(b, s, d)
```

```python
import functools
import math

import jax
import jax.numpy as jnp
import numpy as np
from jax import lax
from jax.experimental import pallas as pl
from jax.experimental.pallas import tpu as pltpu

D_MODEL = 1024
N_HEADS = 16
N_KV_HEADS = 4
HEAD_DIM = D_MODEL // N_HEADS
GROUP = N_HEADS // N_KV_HEADS
ROT_DIM = HEAD_DIM // 4
ROT_HALF = ROT_DIM // 2
ROPE_THETA = 500000.0
WINDOW = 128
BLK = 128
Q_DIM = N_HEADS * HEAD_DIM
KV_DIM = N_KV_HEADS * HEAD_DIM
QKV_DIM = Q_DIM + 2 * KV_DIM
CONV_CH = D_MODEL
CONV_WIDTH = 31
CONV_PAD = (CONV_WIDTH - 1) // 2
D_FF = ((8 * D_MODEL // 3 + 255) // 256) * 256
EPS = 1e-6
NEG = -1e30

LANES = 128
SUBLANES = 8
VMEM_LIMIT_BYTES = 56 * 1024 * 1024

TOK_TILE = 512
QKV_TILE = 1024
FF_CHUNK = 256
HALO = 2 * SUBLANES
CONV_ROWS = 64
N_LANE_TILES = D_MODEL // LANES
HEADS_PER_TILE = LANES // HEAD_DIM
PAIRS_PER_GROUP = GROUP // HEADS_PER_TILE
ATT_QBLKS = TOK_TILE // BLK
ATT_KBLKS = ATT_QBLKS + 2
ATT_VALUE_LAG = 1
ATT_P_SLOTS = 2 * (ATT_VALUE_LAG + 1)
PAIR_ROWS = HEADS_PER_TILE * BLK

_f32 = jnp.float32
_bf16 = jnp.bfloat16


def _rmsnorm(xf, g):
    y = xf * lax.rsqrt(jnp.mean(xf * xf, axis=-1, keepdims=True) + EPS)
    return y * g


def _const_spec(shape, index=None):
    index = (0,) * len(shape) if index is None else index
    return pl.BlockSpec(shape, lambda *_: index, pipeline_mode=pl.Buffered(1))


def _compiler_params(n_axes):
    return pltpu.CompilerParams(
        dimension_semantics=("arbitrary",) * n_axes,
        vmem_limit_bytes=VMEM_LIMIT_BYTES,
    )


def _qkv_kernel(x_ref, g_ref, w_ref, cos_ref, sin_hi_ref, sin_lo_ref, q_ref, k_ref, v_ref):
    h = _rmsnorm(x_ref[...], g_ref[...]).astype(_bf16)
    qkv = jnp.dot(h, w_ref[...], preferred_element_type=_f32)
    cos = cos_ref[...]
    sin_hi = sin_hi_ref[...]
    sin_lo = sin_lo_ref[...]
    scale = 1.0 / math.sqrt(HEAD_DIM)

    def rope(t):
        return (t * cos + pltpu.roll(t, ROT_HALF, 1) * sin_hi
                + pltpu.roll(t, LANES - ROT_HALF, 1) * sin_lo)

    for j in range(Q_DIM // LANES):
        t = qkv[:, j * LANES:(j + 1) * LANES]
        q_ref[:, j * LANES:(j + 1) * LANES] = (rope(t) * scale).astype(_bf16)
    for j in range(KV_DIM // LANES):
        t = qkv[:, Q_DIM + j * LANES:Q_DIM + (j + 1) * LANES]
        k_ref[:, j * LANES:(j + 1) * LANES] = rope(t).astype(_bf16)
    v_ref[...] = qkv[:, Q_DIM + KV_DIM:].astype(_bf16)


def _qkv_call(x2, g, w_qkv, cos_t, sin_hi_t, sin_lo_t, seq):
    m = x2.shape[0]
    tiles_per_seq = seq // QKV_TILE
    tok = lambda i: (i, 0)
    pos = lambda i: (i % tiles_per_seq, 0)
    return pl.pallas_call(
        _qkv_kernel,
        grid=(m // QKV_TILE,),
        in_specs=[
            pl.BlockSpec((QKV_TILE, D_MODEL), tok),
            _const_spec((1, D_MODEL)),
            _const_spec((D_MODEL, QKV_DIM)),
            pl.BlockSpec((QKV_TILE, LANES), pos),
            pl.BlockSpec((QKV_TILE, LANES), pos),
            pl.BlockSpec((QKV_TILE, LANES), pos),
        ],
        out_specs=[
            pl.BlockSpec((QKV_TILE, Q_DIM), tok),
            pl.BlockSpec((QKV_TILE, KV_DIM), tok),
            pl.BlockSpec((QKV_TILE, KV_DIM), tok),
        ],
        out_shape=[
            jax.ShapeDtypeStruct((m, Q_DIM), _bf16),
            jax.ShapeDtypeStruct((m, KV_DIM), _bf16),
            jax.ShapeDtypeStruct((m, KV_DIM), _bf16),
        ],
        compiler_params=_compiler_params(1),
        name="qkv_rope",
    )(x2, g, w_qkv, cos_t, sin_hi_t, sin_lo_t)


def _spread(thunks, n_slots):
    bounds = [(len(thunks) * i) // n_slots for i in range(n_slots + 1)]
    return [thunks[bounds[i]:bounds[i + 1]] for i in range(n_slots)]


def _run_fillers(group, produced):
    cache = {}

    def zero_row(r):
        r = r // SUBLANES * SUBLANES
        if r not in cache:
            cache[r] = jnp.minimum(jnp.abs(produced[r:r + SUBLANES, :LANES]), 0.0)
        return cache[r][:1, :]

    n_rows = produced.shape[0]
    for i, thunk in enumerate(group):
        thunk(zero_row, (i * n_rows) // len(group), ((i + 1) * n_rows) // len(group))


def _run_ungated(group):
    zero = jnp.zeros((1, LANES), _f32)
    for thunk in group:
        thunk(lambda r: zero, 0, 0)


def _ffn_residual(xf, g, wgu_ref, wd_ref, act_ref, fillers, fillers_have_matmuls):
    n_up = D_FF // FF_CHUNK
    n_down = D_MODEL // FF_CHUNK
    groups = _spread(list(fillers), n_up + n_down)
    h = _rmsnorm(xf, g).astype(_bf16)
    previous = xf
    for c in range(n_up):
        lo = c * FF_CHUNK
        gate = jnp.dot(h, wgu_ref[:, lo:lo + FF_CHUNK], preferred_element_type=_f32)
        up = jnp.dot(h, wgu_ref[:, D_FF + lo:D_FF + lo + FF_CHUNK], preferred_element_type=_f32)
        act = gate * jax.nn.sigmoid(gate) * up
        act_ref[:, lo:lo + FF_CHUNK] = act.astype(_bf16)
        _run_fillers(groups[c], previous if fillers_have_matmuls else act)
        previous = act
    outs = []
    for c in range(n_down):
        lo = c * FF_CHUNK
        out = xf[:, lo:lo + FF_CHUNK] + jnp.dot(act_ref[...], wd_ref[:, lo:lo + FF_CHUNK],
                                                 preferred_element_type=_f32)
        outs.append(out)
        _run_fillers(groups[n_up + c], previous if fillers_have_matmuls else out)
        previous = out
    return jnp.concatenate(outs, axis=1)


def _stage_kv_stacks(kp_ref, kc_ref, kn_ref, vp_ref, vc_ref, vn_ref, kz_ref, vz_ref):
    lane = lax.broadcasted_iota(jnp.int32, (BLK, LANES), 1)
    low_lanes = lane < HEAD_DIM
    zero = jnp.zeros((), _bf16)
    high_flag = (lane // HEAD_DIM).astype(_f32)
    ones_low = (1.0 - high_flag).astype(_bf16)
    ones_high = high_flag.astype(_bf16)
    for kb in range(ATT_KBLKS):
        if kb == 0:
            k_blk, v_blk = kp_ref[...], vp_ref[...]
        elif kb == ATT_KBLKS - 1:
            k_blk, v_blk = kn_ref[...], vn_ref[...]
        else:
            rows = slice((kb - 1) * BLK, kb * BLK)
            k_blk, v_blk = kc_ref[rows, :], vc_ref[rows, :]
        for g in range(N_KV_HEADS):
            tile = slice((g // HEADS_PER_TILE) * LANES, (g // HEADS_PER_TILE + 1) * LANES)
            own = low_lanes if g % HEADS_PER_TILE == 0 else ~low_lanes
            k_own = jnp.where(own, k_blk[:, tile], zero)
            v_own = jnp.where(own, v_blk[:, tile], zero)
            k_swap = pltpu.roll(k_own, HEAD_DIM, 1)
            v_swap = pltpu.roll(v_own, HEAD_DIM, 1)
            if g % HEADS_PER_TILE == 0:
                k_lo, k_hi, v_lo, v_hi = k_own, k_swap, v_own, v_swap
            else:
                k_lo, k_hi, v_lo, v_hi = k_swap, k_own, v_swap, v_own
            base = kb * PAIR_ROWS
            kz_ref[g, base:base + BLK, :] = k_lo
            kz_ref[g, base + BLK:base + PAIR_ROWS, :] = k_hi
            vz_ref[g, base:base + BLK, :LANES] = v_lo
            vz_ref[g, base + BLK:base + PAIR_ROWS, :LANES] = v_hi
            vz_ref[g, base:base + BLK, LANES:] = ones_low
            vz_ref[g, base + BLK:base + PAIR_ROWS, LANES:] = ones_high


def _attn_group_unit(j, g, prev_ok, next_ok, low_lanes_f32, sink_ref, q_ref, kz_ref, vz_ref, o_ref,
                     p_ref, sink_term_ref):
    slot = (j * N_KV_HEADS + g) % ATT_P_SLOTS
    rows = slice(j * BLK, (j + 1) * BLK)
    stack = slice(j * PAIR_ROWS, (j + 3) * PAIR_ROWS)
    q_tiles = [g * PAIRS_PER_GROUP + pair for pair in range(PAIRS_PER_GROUP)]

    def scores(zero_row, row_lo, row_hi):
        del row_hi
        q2 = jnp.concatenate([q_ref[rows, qt * LANES:(qt + 1) * LANES] for qt in q_tiles], axis=0)
        q2 = q2 + zero_row(row_lo).astype(_bf16)
        s_all = lax.dot_general(q2, kz_ref[g, stack, :], (((1,), (1,)), ((), ())),
                                preferred_element_type=_f32)
        for pair, qt in enumerate(q_tiles):
            s = s_all[pair * BLK:(pair + 1) * BLK, :]
            probs = [None] * (3 * HEADS_PER_TILE)
            sink_terms = []
            for half in range(HEADS_PER_TILE):
                sink = sink_ref[qt * HEADS_PER_TILE + half]
                col = lambda kb: slice((kb * HEADS_PER_TILE + half) * BLK, (kb * HEADS_PER_TILE + half + 1) * BLK)
                s_prev = jnp.where(prev_ok, s[:, col(0)], NEG)
                s_own = s[:, col(1)]
                s_next = jnp.where(next_ok, s[:, col(2)], NEG)
                mx = jnp.max(jnp.maximum(jnp.maximum(s_prev, s_own), s_next), axis=-1, keepdims=True)
                mx = jnp.maximum(mx, sink)
                probs[0 * HEADS_PER_TILE + half] = jnp.exp(s_prev - mx).astype(_bf16)
                probs[1 * HEADS_PER_TILE + half] = jnp.exp(s_own - mx).astype(_bf16)
                probs[2 * HEADS_PER_TILE + half] = jnp.exp(s_next - mx).astype(_bf16)
                sink_terms.append(jnp.exp(sink - mx))
            p_ref[slot, pair * BLK:(pair + 1) * BLK, :] = jnp.concatenate(probs, axis=1)
            sink_term_ref[slot, pair * BLK:(pair + 1) * BLK, :] = jnp.where(low_lanes_f32, sink_terms[0],
                                                                            sink_terms[1])

    def values(zero_row, row_lo, row_hi):
        del zero_row, row_lo, row_hi
        o_ext = jnp.dot(p_ref[slot], vz_ref[g, stack, :], preferred_element_type=_f32)
        o_all = o_ext[:, :LANES] / (o_ext[:, LANES:] + sink_term_ref[slot])
        for pair, qt in enumerate(q_tiles):
            o_ref[rows, qt * LANES:(qt + 1) * LANES] = o_all[pair * BLK:(pair + 1) * BLK, :].astype(_bf16)

    return scores, values


def _attn_ffn_kernel(sink_ref, q_ref, kp_ref, kc_ref, kn_ref, vp_ref, vc_ref, vn_ref, xb_ref,
                     wo_ref, g_ref, wgu_ref, wd_ref, gc_ref, w1_ref, b1_ref, out_ref, u_out_ref,
                     kz_ref, vz_ref, o_ref, p_ref, sink_term_ref, act_ref, *, n_tiles, tiles_per_seq):
    s = pl.program_id(0)

    def front_units():
        _stage_kv_stacks(kp_ref, kc_ref, kn_ref, vp_ref, vc_ref, vn_ref, kz_ref, vz_ref)
        has_prev = (s % tiles_per_seq) != 0
        has_next = (s % tiles_per_seq) != tiles_per_seq - 1
        qi = lax.broadcasted_iota(jnp.int32, (BLK, BLK), 0)
        ci = lax.broadcasted_iota(jnp.int32, (BLK, BLK), 1)
        prev_in_window = ci >= qi
        next_in_window = ci <= qi
        low_lanes_f32 = (lax.broadcasted_iota(jnp.int32, (BLK, LANES), 1) & HEAD_DIM) == 0
        halves = []
        for j in range(ATT_QBLKS):
            prev_ok = prev_in_window & has_prev if j == 0 else prev_in_window
            next_ok = next_in_window & has_next if j == ATT_QBLKS - 1 else next_in_window
            for g in range(N_KV_HEADS):
                halves.append(_attn_group_unit(j, g, prev_ok, next_ok, low_lanes_f32, sink_ref, q_ref,
                                               kz_ref, vz_ref, o_ref, p_ref, sink_term_ref))
        units = []
        for i in range(len(halves) + ATT_VALUE_LAG):
            if i < len(halves):
                units.append(halves[i][0])
            if i >= ATT_VALUE_LAG:
                units.append(halves[i - ATT_VALUE_LAG][1])
        return units

    def back_half(units):
        x1 = xb_ref[...] + jnp.dot(o_ref[...], wo_ref[...], preferred_element_type=_f32)
        units = units() if callable(units) else units
        x2 = _ffn_residual(x1, g_ref[...], wgu_ref, wd_ref, act_ref, units, fillers_have_matmuls=True)
        out_ref[...] = x2
        h = _rmsnorm(x2, gc_ref[...]).astype(_bf16)
        for c in range(CONV_CH // FF_CHUNK):
            lo = c * FF_CHUNK
            val = (jnp.dot(h, w1_ref[:, lo:lo + FF_CHUNK], preferred_element_type=_f32)
                   + b1_ref[:, lo:lo + FF_CHUNK])
            gate = (jnp.dot(h, w1_ref[:, CONV_CH + lo:CONV_CH + lo + FF_CHUNK], preferred_element_type=_f32)
                    + b1_ref[:, CONV_CH + lo:CONV_CH + lo + FF_CHUNK])
            u_out_ref[:, lo:lo + FF_CHUNK] = val * jax.nn.sigmoid(gate)

    @pl.when(s == 0)
    def _():
        _run_ungated(front_units())

    @pl.when((s > 0) & (s < n_tiles))
    def _():
        back_half(front_units)

    @pl.when(s == n_tiles)
    def _():
        back_half([])


def _attn_ffn_call(x2, q, k, v, sink, w_o, g, w_gu, w_d, gc, w1, b1, seq):
    m = x2.shape[0]
    n_tiles = m // TOK_TILE
    tiles_per_seq = seq // TOK_TILE
    n_blk = m // BLK
    front = lambda s: jnp.minimum(s, n_tiles - 1)
    back = lambda s: jnp.maximum(s - 1, 0)
    tok_front = lambda s: (front(s), 0)
    tok_back = lambda s: (back(s), 0)
    prev = lambda s: (jnp.maximum(front(s) * ATT_QBLKS - 1, 0), 0)
    nxt = lambda s: (jnp.minimum((front(s) + 1) * ATT_QBLKS, n_blk - 1), 0)
    edge_spec = lambda imap: pl.BlockSpec((BLK, KV_DIM), imap)
    own_spec = pl.BlockSpec((TOK_TILE, KV_DIM), tok_front)
    layer0 = lambda shape: _const_spec((None,) + shape, (0,) * (len(shape) + 1))
    return pl.pallas_call(
        functools.partial(_attn_ffn_kernel, n_tiles=n_tiles, tiles_per_seq=tiles_per_seq),
        grid=(n_tiles + 1,),
        in_specs=[
            pl.BlockSpec(memory_space=pltpu.SMEM),
            pl.BlockSpec((TOK_TILE, Q_DIM), tok_front),
            edge_spec(prev), own_spec, edge_spec(nxt),
            edge_spec(prev), own_spec, edge_spec(nxt),
            pl.BlockSpec((TOK_TILE, D_MODEL), tok_back),
            _const_spec((Q_DIM, D_MODEL)),
            _const_spec((1, D_MODEL)),
            layer0((D_MODEL, 2 * D_FF)),
            layer0((D_FF, D_MODEL)),
            _const_spec((1, D_MODEL)),
            _const_spec((D_MODEL, 2 * CONV_CH)),
            _const_spec((1, 2 * CONV_CH)),
        ],
        out_specs=[
            pl.BlockSpec((TOK_TILE, D_MODEL), tok_back),
            pl.BlockSpec((TOK_TILE, CONV_CH), tok_back),
        ],
        out_shape=[
            jax.ShapeDtypeStruct((m, D_MODEL), _f32),
            jax.ShapeDtypeStruct((m, CONV_CH), _f32),
        ],
        scratch_shapes=[
            pltpu.VMEM((N_KV_HEADS, ATT_KBLKS * PAIR_ROWS, LANES), _bf16),
            pltpu.VMEM((N_KV_HEADS, ATT_KBLKS * PAIR_ROWS, 2 * LANES), _bf16),
            pltpu.VMEM((TOK_TILE, Q_DIM), _bf16),
            pltpu.VMEM((ATT_P_SLOTS, PAIRS_PER_GROUP * BLK, 3 * PAIR_ROWS), _bf16),
            pltpu.VMEM((ATT_P_SLOTS, PAIRS_PER_GROUP * BLK, LANES), _f32),
            pltpu.VMEM((TOK_TILE, D_FF), _bf16),
        ],
        compiler_params=_compiler_params(1),
        name="attn_ffn",
    )(sink, q, k, k, k, v, v, v, x2, w_o, g, w_gu, w_d, gc, w1, b1)


def _conv_ffn_kernel(ua_ref, up_ref, un_ref, xb_ref, wdw_ref, bdw_ref, lng_ref, lnb_ref,
                     w2_ref, b2_ref, gf_ref, wgu_ref, wd_ref, gfin_ref, out_ref,
                     u_ref, c_ref, y_ref, act_ref, *, n_tiles, tiles_per_seq):
    s = pl.program_id(0)

    @pl.when(s == 0)
    def _():
        y_ref[...] = jnp.zeros_like(y_ref)

    t_idx = jnp.minimum(s, n_tiles - 1)
    has_prev = (t_idx % tiles_per_seq) != 0
    has_next = (t_idx % tiles_per_seq) != tiles_per_seq - 1
    u_prev = jnp.where(has_prev, up_ref[...], 0.0)
    u_next = jnp.where(has_next, un_ref[...], 0.0)
    for t in range(N_LANE_TILES):
        tile = slice(t * LANES, (t + 1) * LANES)
        u_ref[t, :HALO, :] = u_prev[:, tile]
        u_ref[t, HALO:HALO + TOK_TILE, :] = ua_ref[:, tile]
        u_ref[t, HALO + TOK_TILE:, :] = u_next[:, tile]

    def conv_unit(t, rc):
        def run(zero_row, row_lo, row_hi):
            acc = None
            for j in range(CONV_WIDTH):
                start = rc * CONV_ROWS + HALO - CONV_PAD + j
                w_rows = wdw_ref[t, j] + zero_row(row_lo + (j * (row_hi - row_lo)) // CONV_WIDTH)
                taps = u_ref[t, pl.ds(start, CONV_ROWS), :].reshape(CONV_ROWS // SUBLANES, SUBLANES, LANES)
                prod = taps * w_rows[None]
                acc = prod if acc is None else acc + prod
            c_ref[t, pl.ds(rc * CONV_ROWS, CONV_ROWS), :] = acc.reshape(CONV_ROWS, LANES) + bdw_ref[t]
        return run

    def norm_unit(rc):
        def run(zero_row, row_lo, row_hi):
            del zero_row, row_lo, row_hi
            rsl = pl.ds(rc * CONV_ROWS, CONV_ROWS)
            c = jnp.concatenate([c_ref[t, rsl, :] for t in range(N_LANE_TILES)], axis=1)
            mu = jnp.mean(c, axis=-1, keepdims=True)
            var = jnp.mean(jnp.square(c - mu), axis=-1, keepdims=True)
            y = (c - mu) * lax.rsqrt(var + EPS) * lng_ref[...] + lnb_ref[...]
            y_ref[rsl, :] = (y * jax.nn.sigmoid(y)).astype(_bf16)
        return run

    n_rc = TOK_TILE // CONV_ROWS
    units = [conv_unit(t, rc) for t in range(N_LANE_TILES) for rc in range(n_rc)]
    units += [norm_unit(rc) for rc in range(n_rc)]

    n_pw2 = D_MODEL // FF_CHUNK
    n_ffn = D_FF // FF_CHUNK + D_MODEL // FF_CHUNK
    n_early = (len(units) * n_pw2) // (n_pw2 + n_ffn)
    pw2_groups = _spread(units[:n_early], n_pw2)
    x1_cols = []
    for c in range(n_pw2):
        lo = c * FF_CHUNK
        x1_c = (xb_ref[:, lo:lo + FF_CHUNK]
                + jnp.dot(y_ref[...], w2_ref[:, lo:lo + FF_CHUNK], preferred_element_type=_f32)
                + b2_ref[:, lo:lo + FF_CHUNK])
        x1_cols.append(x1_c)
        _run_fillers(pw2_groups[c], x1_c)
    x1 = jnp.concatenate(x1_cols, axis=1)
    x2 = _ffn_residual(x1, gf_ref[...], wgu_ref, wd_ref, act_ref, units[n_early:], fillers_have_matmuls=False)
    out_ref[...] = _rmsnorm(x2, gfin_ref[...])


def _conv_ffn_call(x2, u, wdw, bdw, lng, lnb, w2, b2, gf, w_gu, w_d, gfin, seq):
    m = x2.shape[0]
    n_tiles = m // TOK_TILE
    tiles_per_seq = seq // TOK_TILE
    halo_per_tile = TOK_TILE // HALO
    n_halo = m // HALO
    front = lambda s: jnp.minimum(s, n_tiles - 1)
    back = lambda s: jnp.maximum(s - 1, 0)
    tok_front = lambda s: (front(s), 0)
    tok_back = lambda s: (back(s), 0)
    prev = lambda s: (jnp.maximum(front(s) * halo_per_tile - 1, 0), 0)
    nxt = lambda s: (jnp.minimum((front(s) + 1) * halo_per_tile, n_halo - 1), 0)
    rows = TOK_TILE + 2 * HALO
    layer1 = lambda shape: _const_spec((None,) + shape, (1,) + (0,) * len(shape))
    return pl.pallas_call(
        functools.partial(_conv_ffn_kernel, n_tiles=n_tiles, tiles_per_seq=tiles_per_seq),
        grid=(n_tiles + 1,),
        in_specs=[
            pl.BlockSpec((TOK_TILE, CONV_CH), tok_front),
            pl.BlockSpec((HALO, CONV_CH), prev),
            pl.BlockSpec((HALO, CONV_CH), nxt),
            pl.BlockSpec((TOK_TILE, D_MODEL), tok_back),
            _const_spec((N_LANE_TILES, CONV_WIDTH, SUBLANES, LANES)),
            _const_spec((N_LANE_TILES, 1, LANES)),
            _const_spec((1, CONV_CH)),
            _const_spec((1, CONV_CH)),
            _const_spec((CONV_CH, D_MODEL)),
            _const_spec((1, D_MODEL)),
            _const_spec((1, D_MODEL)),
            layer1((D_MODEL, 2 * D_FF)),
            layer1((D_FF, D_MODEL)),
            _const_spec((1, D_MODEL)),
        ],
        out_specs=pl.BlockSpec((TOK_TILE, D_MODEL), tok_back),
        out_shape=jax.ShapeDtypeStruct((m, D_MODEL), _f32),
        scratch_shapes=[
            pltpu.VMEM((N_LANE_TILES, rows, LANES), _f32),
            pltpu.VMEM((N_LANE_TILES, TOK_TILE, LANES), _f32),
            pltpu.VMEM((TOK_TILE, CONV_CH), _bf16),
            pltpu.VMEM((TOK_TILE, D_FF), _bf16),
        ],
        compiler_params=_compiler_params(1),
        name="conv_ffn",
    )(u, u, u, x2, wdw, bdw, lng, lnb, w2, b2, gf, w_gu, w_d, gfin)


def _rope_tables(seq):
    pos = np.arange(seq, dtype=np.float64)
    inv_freq = ROPE_THETA ** (-np.arange(0, ROT_DIM, 2, dtype=np.float64) / ROT_DIM)
    ang = pos[:, None] * inv_freq[None, :]
    cos = np.cos(ang)
    sin = np.sin(ang)
    ones = np.ones((seq, HEAD_DIM - ROT_DIM))
    zeros_half = np.zeros((seq, ROT_HALF))
    zeros_rest = np.zeros((seq, HEAD_DIM - ROT_DIM))
    cos_h = np.concatenate([cos, cos, ones], axis=1)
    sin_hi_h = np.concatenate([zeros_half, sin, zeros_rest], axis=1)
    sin_lo_h = np.concatenate([-sin, zeros_half, zeros_rest], axis=1)
    rep = lambda a: jnp.asarray(np.tile(a, (1, HEADS_PER_TILE)), dtype=_f32)
    return rep(cos_h), rep(sin_hi_h), rep(sin_lo_h)


def kernel(x, attn_norm, attn_w_qkv, attn_w_o, attn_sink, conv_norm, conv_w_pw1, conv_b_pw1, conv_w_dw,
           conv_b_dw, conv_ln_g, conv_ln_b, conv_w_pw2, conv_b_pw2, ffn_norm, ffn_w_gu, ffn_w_down, final_norm):
    b, s, d = x.shape
    assert d == D_MODEL and s % TOK_TILE == 0 and s % QKV_TILE == 0
    assert attn_norm.shape[0] == 1 and conv_norm.shape[0] == 1 and ffn_norm.shape[0] == 2
    m = b * s
    x2 = x.reshape(m, d)
    row = lambda v: v.reshape(1, -1).astype(_f32)
    bf = lambda w: w.astype(_bf16)
    w_gu = bf(ffn_w_gu)
    w_d = bf(ffn_w_down)

    cos_t, sin_hi_t, sin_lo_t = _rope_tables(s)
    q, k, v = _qkv_call(x2, row(attn_norm[0]), bf(attn_w_qkv[0]), cos_t, sin_hi_t, sin_lo_t, s)
    x2, u = _attn_ffn_call(x2, q, k, v, attn_sink[0].astype(_f32), bf(attn_w_o[0]), row(ffn_norm[0]),
                           w_gu, w_d, row(conv_norm[0]), bf(conv_w_pw1[0]), row(conv_b_pw1[0]), s)

    wdw = conv_w_dw[0].reshape(CONV_WIDTH, N_LANE_TILES, LANES).transpose(1, 0, 2)
    wdw = jnp.broadcast_to(wdw[:, :, None, :], (N_LANE_TILES, CONV_WIDTH, SUBLANES, LANES))
    bdw = conv_b_dw[0].reshape(N_LANE_TILES, 1, LANES)
    out = _conv_ffn_call(x2, u, wdw, bdw, row(conv_ln_g[0]), row(conv_ln_b[0]), bf(conv_w_pw2[0]),
                         row(conv_b_pw2[0]), row(ffn_norm[1]), w_gu, w_d, row(final_norm), s)
    return out.reshape(b, s, d)
```

```python
import functools
import math

import jax
import jax.numpy as jnp
import numpy as np
from jax import lax
from jax.experimental import pallas as pl
from jax.experimental.pallas import tpu as pltpu

D_MODEL = 1024
N_HEADS = 16
N_KV_HEADS = 4
HEAD_DIM = D_MODEL // N_HEADS
GROUP = N_HEADS // N_KV_HEADS
ROT_DIM = HEAD_DIM // 4
ROT_HALF = ROT_DIM // 2
ROPE_THETA = 500000.0
WINDOW = 128
BLK = 128
Q_DIM = N_HEADS * HEAD_DIM
KV_DIM = N_KV_HEADS * HEAD_DIM
QKV_DIM = Q_DIM + 2 * KV_DIM
CONV_CH = D_MODEL
CONV_WIDTH = 31
CONV_PAD = (CONV_WIDTH - 1) // 2
D_FF = ((8 * D_MODEL // 3 + 255) // 256) * 256
EPS = 1e-6
NEG = -1e30

LANES = 128
SUBLANES = 8
VMEM_LIMIT_BYTES = 56 * 1024 * 1024

TOK_TILE = 512
QKV_TILE = 1024
FF_CHUNK = 256
HALO = 2 * SUBLANES
CONV_ROWS = 64
N_LANE_TILES = D_MODEL // LANES
HEADS_PER_TILE = LANES // HEAD_DIM
PAIRS_PER_GROUP = GROUP // HEADS_PER_TILE
ATT_QBLKS = TOK_TILE // BLK
ATT_KBLKS = ATT_QBLKS + 2
ATT_VALUE_LAG = 2
ATT_P_SLOTS = 2 * (ATT_VALUE_LAG + 1)
PAIR_ROWS = HEADS_PER_TILE * BLK

_f32 = jnp.float32
_bf16 = jnp.bfloat16


def _rmsnorm(xf, g):
    y = xf * lax.rsqrt(jnp.mean(xf * xf, axis=-1, keepdims=True) + EPS)
    return y * g


def _const_spec(shape, index=None):
    index = (0,) * len(shape) if index is None else index
    return pl.BlockSpec(shape, lambda *_: index, pipeline_mode=pl.Buffered(1))


def _compiler_params(n_axes, n_inputs=None, cast_inputs=()):
    fusion = None if n_inputs is None else [i in cast_inputs for i in range(n_inputs)]
    return pltpu.CompilerParams(
        dimension_semantics=("arbitrary",) * n_axes,
        vmem_limit_bytes=VMEM_LIMIT_BYTES,
        allow_input_fusion=fusion,
    )


def _qkv_kernel(x_ref, g_ref, w_ref, cos_ref, sin_hi_ref, sin_lo_ref, q_ref, k_ref, v_ref):
    h = _rmsnorm(x_ref[...], g_ref[...]).astype(_bf16)
    qkv = jnp.dot(h, w_ref[...], preferred_element_type=_f32)
    cos = cos_ref[...]
    sin_hi = sin_hi_ref[...]
    sin_lo = sin_lo_ref[...]
    scale = 1.0 / math.sqrt(HEAD_DIM)

    def rope(t):
        return (t * cos + pltpu.roll(t, ROT_HALF, 1) * sin_hi
                + pltpu.roll(t, LANES - ROT_HALF, 1) * sin_lo)

    for j in range(Q_DIM // LANES):
        t = qkv[:, j * LANES:(j + 1) * LANES]
        q_ref[:, j * LANES:(j + 1) * LANES] = (rope(t) * scale).astype(_bf16)
    for j in range(KV_DIM // LANES):
        t = qkv[:, Q_DIM + j * LANES:Q_DIM + (j + 1) * LANES]
        k_ref[:, j * LANES:(j + 1) * LANES] = rope(t).astype(_bf16)
    v_ref[...] = qkv[:, Q_DIM + KV_DIM:].astype(_bf16)


def _qkv_call(x2, g, w_qkv, cos_t, sin_hi_t, sin_lo_t, seq):
    m = x2.shape[0]
    tiles_per_seq = seq // QKV_TILE
    tok = lambda i: (i, 0)
    pos = lambda i: (i % tiles_per_seq, 0)
    return pl.pallas_call(
        _qkv_kernel,
        grid=(m // QKV_TILE,),
        in_specs=[
            pl.BlockSpec((QKV_TILE, D_MODEL), tok),
            _const_spec((1, D_MODEL)),
            _const_spec((D_MODEL, QKV_DIM)),
            pl.BlockSpec((QKV_TILE, LANES), pos),
            pl.BlockSpec((QKV_TILE, LANES), pos),
            pl.BlockSpec((QKV_TILE, LANES), pos),
        ],
        out_specs=[
            pl.BlockSpec((QKV_TILE, Q_DIM), tok),
            pl.BlockSpec((QKV_TILE, KV_DIM), tok),
            pl.BlockSpec((QKV_TILE, KV_DIM), tok),
        ],
        out_shape=[
            jax.ShapeDtypeStruct((m, Q_DIM), _bf16),
            jax.ShapeDtypeStruct((m, KV_DIM), _bf16),
            jax.ShapeDtypeStruct((m, KV_DIM), _bf16),
        ],
        compiler_params=_compiler_params(1),
        name="qkv_rope",
    )(x2, g, w_qkv, cos_t, sin_hi_t, sin_lo_t)


def _spread(thunks, n_slots):
    bounds = [(len(thunks) * i) // n_slots for i in range(n_slots + 1)]
    return [thunks[bounds[i]:bounds[i + 1]] for i in range(n_slots)]


def _run_fillers(group, produced):
    cache = {}

    def zero_row(r):
        r = r // SUBLANES * SUBLANES
        if r not in cache:
            cache[r] = jnp.minimum(jnp.abs(produced[r:r + SUBLANES, :LANES]), 0.0)
        return cache[r][:1, :]

    n_rows = produced.shape[0]
    for i, thunk in enumerate(group):
        thunk(zero_row, (i * n_rows) // len(group), ((i + 1) * n_rows) // len(group))


def _run_ungated(group):
    zero = jnp.zeros((1, LANES), _f32)
    for thunk in group:
        thunk(lambda r: zero, 0, 0)


def _ffn_residual(xf, g, wgu_ref, wd_ref, act_ref, fillers, fillers_have_matmuls):
    n_up = D_FF // FF_CHUNK
    n_down = D_MODEL // FF_CHUNK
    groups = _spread(list(fillers), n_up + n_down)
    h = _rmsnorm(xf, g).astype(_bf16)
    previous = xf
    for c in range(n_up):
        lo = c * FF_CHUNK
        gate = jnp.dot(h, wgu_ref[:, lo:lo + FF_CHUNK], preferred_element_type=_f32)
        up = jnp.dot(h, wgu_ref[:, D_FF + lo:D_FF + lo + FF_CHUNK], preferred_element_type=_f32)
        act = gate * jax.nn.sigmoid(gate) * up
        act_ref[:, lo:lo + FF_CHUNK] = act.astype(_bf16)
        _run_fillers(groups[c], previous if fillers_have_matmuls else act)
        previous = act
    outs = []
    for c in range(n_down):
        lo = c * FF_CHUNK
        out = xf[:, lo:lo + FF_CHUNK] + jnp.dot(act_ref[...], wd_ref[:, lo:lo + FF_CHUNK],
                                                 preferred_element_type=_f32)
        outs.append(out)
        _run_fillers(groups[n_up + c], previous if fillers_have_matmuls else out)
        previous = out
    return jnp.concatenate(outs, axis=1)


def _stage_kv_stacks(kp_ref, kc_ref, kn_ref, vp_ref, vc_ref, vn_ref, kz_ref, vz_ref):
    lane = lax.broadcasted_iota(jnp.int32, (BLK, LANES), 1)
    low_lanes = lane < HEAD_DIM
    zero = jnp.zeros((), _bf16)
    high_flag = (lane // HEAD_DIM).astype(_f32)
    ones_low = (1.0 - high_flag).astype(_bf16)
    ones_high = high_flag.astype(_bf16)
    for kb in range(ATT_KBLKS):
        if kb == 0:
            k_blk, v_blk = kp_ref[...], vp_ref[...]
        elif kb == ATT_KBLKS - 1:
            k_blk, v_blk = kn_ref[...], vn_ref[...]
        else:
            rows = slice((kb - 1) * BLK, kb * BLK)
            k_blk, v_blk = kc_ref[rows, :], vc_ref[rows, :]
        for g in range(N_KV_HEADS):
            tile = slice((g // HEADS_PER_TILE) * LANES, (g // HEADS_PER_TILE + 1) * LANES)
            own = low_lanes if g % HEADS_PER_TILE == 0 else ~low_lanes
            k_own = jnp.where(own, k_blk[:, tile], zero)
            v_own = jnp.where(own, v_blk[:, tile], zero)
            k_swap = pltpu.roll(k_own, HEAD_DIM, 1)
            v_swap = pltpu.roll(v_own, HEAD_DIM, 1)
            if g % HEADS_PER_TILE == 0:
                k_lo, k_hi, v_lo, v_hi = k_own, k_swap, v_own, v_swap
            else:
                k_lo, k_hi, v_lo, v_hi = k_swap, k_own, v_swap, v_own
            base = kb * PAIR_ROWS
            kz_ref[g, base:base + BLK, :] = k_lo
            kz_ref[g, base + BLK:base + PAIR_ROWS, :] = k_hi
            vz_ref[g, base:base + BLK, :LANES] = v_lo
            vz_ref[g, base + BLK:base + PAIR_ROWS, :LANES] = v_hi
            vz_ref[g, base:base + BLK, LANES:] = ones_low
            vz_ref[g, base + BLK:base + PAIR_ROWS, LANES:] = ones_high


def _attn_group_unit(j, g, prev_ok, next_ok, low_lanes_f32, sink_ref, q_ref, kz_ref, vz_ref, o_ref,
                     p_ref, sink_term_ref):
    slot = (j * N_KV_HEADS + g) % ATT_P_SLOTS
    rows = slice(j * BLK, (j + 1) * BLK)
    stack = slice(j * PAIR_ROWS, (j + 3) * PAIR_ROWS)
    q_tiles = [g * PAIRS_PER_GROUP + pair for pair in range(PAIRS_PER_GROUP)]

    def scores(zero_row, row_lo, row_hi):
        del row_hi
        q2 = jnp.concatenate([q_ref[rows, qt * LANES:(qt + 1) * LANES] for qt in q_tiles], axis=0)
        q2 = q2 + zero_row(row_lo).astype(_bf16)
        s_all = lax.dot_general(q2, kz_ref[g, stack, :], (((1,), (1,)), ((), ())),
                                preferred_element_type=_f32)
        for pair, qt in enumerate(q_tiles):
            s = s_all[pair * BLK:(pair + 1) * BLK, :]
            probs = [None] * (3 * HEADS_PER_TILE)
            sink_terms = []
            for half in range(HEADS_PER_TILE):
                sink = sink_ref[qt * HEADS_PER_TILE + half]
                col = lambda kb: slice((kb * HEADS_PER_TILE + half) * BLK, (kb * HEADS_PER_TILE + half + 1) * BLK)
                s_prev = jnp.where(prev_ok, s[:, col(0)], NEG)
                s_own = s[:, col(1)]
                s_next = jnp.where(next_ok, s[:, col(2)], NEG)
                mx = jnp.max(jnp.maximum(jnp.maximum(s_prev, s_own), s_next), axis=-1, keepdims=True)
                mx = jnp.maximum(mx, sink)
                probs[0 * HEADS_PER_TILE + half] = jnp.exp(s_prev - mx).astype(_bf16)
                probs[1 * HEADS_PER_TILE + half] = jnp.exp(s_own - mx).astype(_bf16)
                probs[2 * HEADS_PER_TILE + half] = jnp.exp(s_next - mx).astype(_bf16)
                sink_terms.append(jnp.exp(sink - mx))
            p_ref[slot, pair * BLK:(pair + 1) * BLK, :] = jnp.concatenate(probs, axis=1)
            sink_term_ref[slot, pair * BLK:(pair + 1) * BLK, :] = jnp.where(low_lanes_f32, sink_terms[0],
                                                                            sink_terms[1])

    def values(zero_row, row_lo, row_hi):
        del zero_row, row_lo, row_hi
        o_ext = jnp.dot(p_ref[slot], vz_ref[g, stack, :], preferred_element_type=_f32)
        o_all = o_ext[:, :LANES] / (o_ext[:, LANES:] + sink_term_ref[slot])
        for pair, qt in enumerate(q_tiles):
            o_ref[rows, qt * LANES:(qt + 1) * LANES] = o_all[pair * BLK:(pair + 1) * BLK, :].astype(_bf16)

    return scores, values


def _attn_ffn_kernel(sink_ref, q_ref, kp_ref, kc_ref, kn_ref, vp_ref, vc_ref, vn_ref, xb_ref,
                     wo_ref, g_ref, wgu_ref, wd_ref, gc_ref, w1_ref, b1_ref, out_ref, u_out_ref,
                     kz_ref, vz_ref, o_ref, p_ref, sink_term_ref, act_ref, *, n_tiles, tiles_per_seq):
    s = pl.program_id(0)

    def front_units():
        _stage_kv_stacks(kp_ref, kc_ref, kn_ref, vp_ref, vc_ref, vn_ref, kz_ref, vz_ref)
        has_prev = (s % tiles_per_seq) != 0
        has_next = (s % tiles_per_seq) != tiles_per_seq - 1
        qi = lax.broadcasted_iota(jnp.int32, (BLK, BLK), 0)
        ci = lax.broadcasted_iota(jnp.int32, (BLK, BLK), 1)
        prev_in_window = ci >= qi
        next_in_window = ci <= qi
        low_lanes_f32 = (lax.broadcasted_iota(jnp.int32, (BLK, LANES), 1) & HEAD_DIM) == 0
        halves = []
        for j in range(ATT_QBLKS):
            prev_ok = prev_in_window & has_prev if j == 0 else prev_in_window
            next_ok = next_in_window & has_next if j == ATT_QBLKS - 1 else next_in_window
            for g in range(N_KV_HEADS):
                halves.append(_attn_group_unit(j, g, prev_ok, next_ok, low_lanes_f32, sink_ref, q_ref,
                                               kz_ref, vz_ref, o_ref, p_ref, sink_term_ref))
        units = []
        for i in range(len(halves) + ATT_VALUE_LAG):
            if i < len(halves):
                units.append(halves[i][0])
            if i >= ATT_VALUE_LAG:
                units.append(halves[i - ATT_VALUE_LAG][1])
        return units

    def back_half(units):
        x1 = xb_ref[...] + jnp.dot(o_ref[...], wo_ref[...], preferred_element_type=_f32)
        units = units() if callable(units) else units
        x2 = _ffn_residual(x1, g_ref[...], wgu_ref, wd_ref, act_ref, units, fillers_have_matmuls=True)
        out_ref[...] = x2
        h = _rmsnorm(x2, gc_ref[...]).astype(_bf16)
        for c in range(CONV_CH // FF_CHUNK):
            lo = c * FF_CHUNK
            val = (jnp.dot(h, w1_ref[:, lo:lo + FF_CHUNK], preferred_element_type=_f32)
                   + b1_ref[:, lo:lo + FF_CHUNK])
            gate = (jnp.dot(h, w1_ref[:, CONV_CH + lo:CONV_CH + lo + FF_CHUNK], preferred_element_type=_f32)
                    + b1_ref[:, CONV_CH + lo:CONV_CH + lo + FF_CHUNK])
            u_out_ref[:, lo:lo + FF_CHUNK] = val * jax.nn.sigmoid(gate)

    @pl.when(s == 0)
    def _():
        _run_ungated(front_units())

    @pl.when((s > 0) & (s < n_tiles))
    def _():
        back_half(front_units)

    @pl.when(s == n_tiles)
    def _():
        back_half([])


def _attn_ffn_call(x2, q, k, v, sink, w_o, g, w_gu, w_d, gc, w1, b1, seq):
    m = x2.shape[0]
    n_tiles = m // TOK_TILE
    tiles_per_seq = seq // TOK_TILE
    n_blk = m // BLK
    front = lambda s: jnp.minimum(s, n_tiles - 1)
    back = lambda s: jnp.maximum(s - 1, 0)
    tok_front = lambda s: (front(s), 0)
    tok_back = lambda s: (back(s), 0)
    prev = lambda s: (jnp.maximum(front(s) * ATT_QBLKS - 1, 0), 0)
    nxt = lambda s: (jnp.minimum((front(s) + 1) * ATT_QBLKS, n_blk - 1), 0)
    edge_spec = lambda imap: pl.BlockSpec((BLK, KV_DIM), imap)
    own_spec = pl.BlockSpec((TOK_TILE, KV_DIM), tok_front)
    layer0 = lambda shape: _const_spec((None,) + shape, (0,) * (len(shape) + 1))
    return pl.pallas_call(
        functools.partial(_attn_ffn_kernel, n_tiles=n_tiles, tiles_per_seq=tiles_per_seq),
        grid=(n_tiles + 1,),
        in_specs=[
            pl.BlockSpec(memory_space=pltpu.SMEM),
            pl.BlockSpec((TOK_TILE, Q_DIM), tok_front),
            edge_spec(prev), own_spec, edge_spec(nxt),
            edge_spec(prev), own_spec, edge_spec(nxt),
            pl.BlockSpec((TOK_TILE, D_MODEL), tok_back),
            _const_spec((Q_DIM, D_MODEL)),
            _const_spec((1, D_MODEL)),
            layer0((D_MODEL, 2 * D_FF)),
            layer0((D_FF, D_MODEL)),
            _const_spec((1, D_MODEL)),
            _const_spec((D_MODEL, 2 * CONV_CH)),
            _const_spec((1, 2 * CONV_CH)),
        ],
        out_specs=[
            pl.BlockSpec((TOK_TILE, D_MODEL), tok_back),
            pl.BlockSpec((TOK_TILE, CONV_CH), tok_back),
        ],
        out_shape=[
            jax.ShapeDtypeStruct((m, D_MODEL), _f32),
            jax.ShapeDtypeStruct((m, CONV_CH), _f32),
        ],
        scratch_shapes=[
            pltpu.VMEM((N_KV_HEADS, ATT_KBLKS * PAIR_ROWS, LANES), _bf16),
            pltpu.VMEM((N_KV_HEADS, ATT_KBLKS * PAIR_ROWS, 2 * LANES), _bf16),
            pltpu.VMEM((TOK_TILE, Q_DIM), _bf16),
            pltpu.VMEM((ATT_P_SLOTS, PAIRS_PER_GROUP * BLK, 3 * PAIR_ROWS), _bf16),
            pltpu.VMEM((ATT_P_SLOTS, PAIRS_PER_GROUP * BLK, LANES), _f32),
            pltpu.VMEM((TOK_TILE, D_FF), _bf16),
        ],
        compiler_params=_compiler_params(1, n_inputs=16, cast_inputs=(9, 11, 12, 14)),
        name="attn_ffn",
    )(sink, q, k, k, k, v, v, v, x2, w_o, g, w_gu, w_d, gc, w1, b1)


def _conv_ffn_kernel(ua_ref, up_ref, un_ref, xb_ref, wdw_ref, bdw_ref, lng_ref, lnb_ref,
                     w2_ref, b2_ref, gf_ref, wgu_ref, wd_ref, gfin_ref, out_ref,
                     u_ref, c_ref, y_ref, act_ref, *, n_tiles, tiles_per_seq):
    s = pl.program_id(0)

    @pl.when(s == 0)
    def _():
        y_ref[...] = jnp.zeros_like(y_ref)

    t_idx = jnp.minimum(s, n_tiles - 1)
    has_prev = (t_idx % tiles_per_seq) != 0
    has_next = (t_idx % tiles_per_seq) != tiles_per_seq - 1
    u_prev = jnp.where(has_prev, up_ref[...], 0.0)
    u_next = jnp.where(has_next, un_ref[...], 0.0)
    for t in range(N_LANE_TILES):
        tile = slice(t * LANES, (t + 1) * LANES)
        u_ref[t, :HALO, :] = u_prev[:, tile]
        u_ref[t, HALO:HALO + TOK_TILE, :] = ua_ref[:, tile]
        u_ref[t, HALO + TOK_TILE:, :] = u_next[:, tile]

    def conv_unit(t, rc):
        def run(zero_row, row_lo, row_hi):
            acc = None
            for j in range(CONV_WIDTH):
                start = rc * CONV_ROWS + HALO - CONV_PAD + j
                w_rows = wdw_ref[t, j] + zero_row(row_lo + (j * (row_hi - row_lo)) // CONV_WIDTH)
                taps = u_ref[t, pl.ds(start, CONV_ROWS), :].reshape(CONV_ROWS // SUBLANES, SUBLANES, LANES)
                prod = taps * w_rows[None]
                acc = prod if acc is None else acc + prod
            c_ref[t, pl.ds(rc * CONV_ROWS, CONV_ROWS), :] = acc.reshape(CONV_ROWS, LANES) + bdw_ref[t]
        return run

    def norm_unit(rc):
        def run(zero_row, row_lo, row_hi):
            del zero_row, row_lo, row_hi
            rsl = pl.ds(rc * CONV_ROWS, CONV_ROWS)
            c = jnp.concatenate([c_ref[t, rsl, :] for t in range(N_LANE_TILES)], axis=1)
            mu = jnp.mean(c, axis=-1, keepdims=True)
            var = jnp.mean(jnp.square(c - mu), axis=-1, keepdims=True)
            y = (c - mu) * lax.rsqrt(var + EPS) * lng_ref[...] + lnb_ref[...]
            y_ref[rsl, :] = (y * jax.nn.sigmoid(y)).astype(_bf16)
        return run

    n_rc = TOK_TILE // CONV_ROWS
    units = [conv_unit(t, rc) for t in range(N_LANE_TILES) for rc in range(n_rc)]
    units += [norm_unit(rc) for rc in range(n_rc)]

    n_pw2 = D_MODEL // FF_CHUNK
    n_ffn = D_FF // FF_CHUNK + D_MODEL // FF_CHUNK
    n_early = (len(units) * n_pw2) // (n_pw2 + n_ffn)
    pw2_groups = _spread(units[:n_early], n_pw2)
    x1_cols = []
    for c in range(n_pw2):
        lo = c * FF_CHUNK
        x1_c = (xb_ref[:, lo:lo + FF_CHUNK]
                + jnp.dot(y_ref[...], w2_ref[:, lo:lo + FF_CHUNK], preferred_element_type=_f32)
                + b2_ref[:, lo:lo + FF_CHUNK])
        x1_cols.append(x1_c)
        _run_fillers(pw2_groups[c], x1_c)
    x1 = jnp.concatenate(x1_cols, axis=1)
    x2 = _ffn_residual(x1, gf_ref[...], wgu_ref, wd_ref, act_ref, units[n_early:], fillers_have_matmuls=False)
    out_ref[...] = _rmsnorm(x2, gfin_ref[...])


def _conv_ffn_call(x2, u, wdw, bdw, lng, lnb, w2, b2, gf, w_gu, w_d, gfin, seq):
    m = x2.shape[0]
    n_tiles = m // TOK_TILE
    tiles_per_seq = seq // TOK_TILE
    halo_per_tile = TOK_TILE // HALO
    n_halo = m // HALO
    front = lambda s: jnp.minimum(s, n_tiles - 1)
    back = lambda s: jnp.maximum(s - 1, 0)
    tok_front = lambda s: (front(s), 0)
    tok_back = lambda s: (back(s), 0)
    prev = lambda s: (jnp.maximum(front(s) * halo_per_tile - 1, 0), 0)
    nxt = lambda s: (jnp.minimum((front(s) + 1) * halo_per_tile, n_halo - 1), 0)
    rows = TOK_TILE + 2 * HALO
    layer1 = lambda shape: _const_spec((None,) + shape, (1,) + (0,) * len(shape))
    return pl.pallas_call(
        functools.partial(_conv_ffn_kernel, n_tiles=n_tiles, tiles_per_seq=tiles_per_seq),
        grid=(n_tiles + 1,),
        in_specs=[
            pl.BlockSpec((TOK_TILE, CONV_CH), tok_front),
            pl.BlockSpec((HALO, CONV_CH), prev),
            pl.BlockSpec((HALO, CONV_CH), nxt),
            pl.BlockSpec((TOK_TILE, D_MODEL), tok_back),
            _const_spec((N_LANE_TILES, CONV_WIDTH, SUBLANES, LANES)),
            _const_spec((N_LANE_TILES, 1, LANES)),
            _const_spec((1, CONV_CH)),
            _const_spec((1, CONV_CH)),
            _const_spec((CONV_CH, D_MODEL)),
            _const_spec((1, D_MODEL)),
            _const_spec((1, D_MODEL)),
            layer1((D_MODEL, 2 * D_FF)),
            layer1((D_FF, D_MODEL)),
            _const_spec((1, D_MODEL)),
        ],
        out_specs=pl.BlockSpec((TOK_TILE, D_MODEL), tok_back),
        out_shape=jax.ShapeDtypeStruct((m, D_MODEL), _f32),
        scratch_shapes=[
            pltpu.VMEM((N_LANE_TILES, rows, LANES), _f32),
            pltpu.VMEM((N_LANE_TILES, TOK_TILE, LANES), _f32),
            pltpu.VMEM((TOK_TILE, CONV_CH), _bf16),
            pltpu.VMEM((TOK_TILE, D_FF), _bf16),
        ],
        compiler_params=_compiler_params(1, n_inputs=14, cast_inputs=(8, 11, 12)),
        name="conv_ffn",
    )(u, u, u, x2, wdw, bdw, lng, lnb, w2, b2, gf, w_gu, w_d, gfin)


def _rope_tables(seq):
    pos = np.arange(seq, dtype=np.float64)
    inv_freq = ROPE_THETA ** (-np.arange(0, ROT_DIM, 2, dtype=np.float64) / ROT_DIM)
    ang = pos[:, None] * inv_freq[None, :]
    cos = np.cos(ang)
    sin = np.sin(ang)
    ones = np.ones((seq, HEAD_DIM - ROT_DIM))
    zeros_half = np.zeros((seq, ROT_HALF))
    zeros_rest = np.zeros((seq, HEAD_DIM - ROT_DIM))
    cos_h = np.concatenate([cos, cos, ones], axis=1)
    sin_hi_h = np.concatenate([zeros_half, sin, zeros_rest], axis=1)
    sin_lo_h = np.concatenate([-sin, zeros_half, zeros_rest], axis=1)
    rep = lambda a: jnp.asarray(np.tile(a, (1, HEADS_PER_TILE)), dtype=_f32)
    return rep(cos_h), rep(sin_hi_h), rep(sin_lo_h)


def kernel(x, attn_norm, attn_w_qkv, attn_w_o, attn_sink, conv_norm, conv_w_pw1, conv_b_pw1, conv_w_dw,
           conv_b_dw, conv_ln_g, conv_ln_b, conv_w_pw2, conv_b_pw2, ffn_norm, ffn_w_gu, ffn_w_down, final_norm):
    b, s, d = x.shape
    assert d == D_MODEL and s % TOK_TILE == 0 and s % QKV_TILE == 0
    assert attn_norm.shape[0] == 1 and conv_norm.shape[0] == 1 and ffn_norm.shape[0] == 2
    m = b * s
    x2 = x.reshape(m, d)
    row = lambda v: v.reshape(1, -1).astype(_f32)
    bf = lambda w: w.astype(_bf16)
    w_gu = bf(ffn_w_gu)
    w_d = bf(ffn_w_down)

    cos_t, sin_hi_t, sin_lo_t = _rope_tables(s)
    q, k, v = _qkv_call(x2, row(attn_norm[0]), bf(attn_w_qkv[0]), cos_t, sin_hi_t, sin_lo_t, s)
    x2, u = _attn_ffn_call(x2, q, k, v, attn_sink[0].astype(_f32), bf(attn_w_o[0]), row(ffn_norm[0]),
                           w_gu, w_d, row(conv_norm[0]), bf(conv_w_pw1[0]), row(conv_b_pw1[0]), s)

    wdw = conv_w_dw[0].reshape(CONV_WIDTH, N_LANE_TILES, LANES).transpose(1, 0, 2)
    wdw = jnp.broadcast_to(wdw[:, :, None, :], (N_LANE_TILES, CONV_WIDTH, SUBLANES, LANES))
    bdw = conv_b_dw[0].reshape(N_LANE_TILES, 1, LANES)
    out = _conv_ffn_call(x2, u, wdw, bdw, row(conv_ln_g[0]), row(conv_ln_b[0]), bf(conv_w_pw2[0]),
                         row(conv_b_pw2[0]), row(ffn_norm[1]), w_gu, w_d, row(final_norm), s)
    return out.reshape(b, s, d)
```

```python
import functools
import math

import jax
import jax.numpy as jnp
import numpy as np
from jax import lax
from jax.experimental import pallas as pl
from jax.experimental.pallas import tpu as pltpu

D_MODEL = 1024
N_HEADS = 16
N_KV_HEADS = 4
HEAD_DIM = D_MODEL // N_HEADS
GROUP = N_HEADS // N_KV_HEADS
ROT_DIM = HEAD_DIM // 4
ROT_HALF = ROT_DIM // 2
ROPE_THETA = 500000.0
WINDOW = 128
BLK = 128
Q_DIM = N_HEADS * HEAD_DIM
KV_DIM = N_KV_HEADS * HEAD_DIM
QKV_DIM = Q_DIM + 2 * KV_DIM
CONV_CH = D_MODEL
CONV_WIDTH = 31
CONV_PAD = (CONV_WIDTH - 1) // 2
D_FF = ((8 * D_MODEL // 3 + 255) // 256) * 256
EPS = 1e-6
NEG = -1e30

LANES = 128
SUBLANES = 8
VMEM_LIMIT_BYTES = 56 * 1024 * 1024

TOK_TILE = 512
QKV_TILE = 1024
FF_CHUNK = 256
HALO = 2 * SUBLANES
CONV_ROWS = 64
N_LANE_TILES = D_MODEL // LANES
HEADS_PER_TILE = LANES // HEAD_DIM
PAIRS_PER_GROUP = GROUP // HEADS_PER_TILE
ATT_QBLKS = TOK_TILE // BLK
ATT_KBLKS = ATT_QBLKS + 2
ATT_VALUE_LAG = 2
ATT_P_SLOTS = 2 * (ATT_VALUE_LAG + 1)
PAIR_ROWS = HEADS_PER_TILE * BLK

_f32 = jnp.float32
_bf16 = jnp.bfloat16


def _rmsnorm(xf, g):
    y = xf * lax.rsqrt(jnp.mean(xf * xf, axis=-1, keepdims=True) + EPS)
    return y * g


def _const_spec(shape, index=None):
    index = (0,) * len(shape) if index is None else index
    return pl.BlockSpec(shape, lambda *_: index, pipeline_mode=pl.Buffered(1))


def _compiler_params(n_axes):
    return pltpu.CompilerParams(
        dimension_semantics=("arbitrary",) * n_axes,
        vmem_limit_bytes=VMEM_LIMIT_BYTES,
    )


def _qkv_kernel(x_ref, g_ref, w_ref, cos_ref, sin_hi_ref, sin_lo_ref, q_ref, k_ref, v_ref):
    h = _rmsnorm(x_ref[...], g_ref[...]).astype(_bf16)
    qkv = jnp.dot(h, w_ref[...], preferred_element_type=_f32)
    cos = cos_ref[...]
    sin_hi = sin_hi_ref[...]
    sin_lo = sin_lo_ref[...]
    scale = 1.0 / math.sqrt(HEAD_DIM)

    def rope(t):
        return (t * cos + pltpu.roll(t, ROT_HALF, 1) * sin_hi
                + pltpu.roll(t, LANES - ROT_HALF, 1) * sin_lo)

    for j in range(Q_DIM // LANES):
        t = qkv[:, j * LANES:(j + 1) * LANES]
        q_ref[:, j * LANES:(j + 1) * LANES] = (rope(t) * scale).astype(_bf16)
    for j in range(KV_DIM // LANES):
        t = qkv[:, Q_DIM + j * LANES:Q_DIM + (j + 1) * LANES]
        k_ref[:, j * LANES:(j + 1) * LANES] = rope(t).astype(_bf16)
    v_ref[...] = qkv[:, Q_DIM + KV_DIM:].astype(_bf16)


def _qkv_call(x2, g, w_qkv, cos_t, sin_hi_t, sin_lo_t, seq):
    m = x2.shape[0]
    tiles_per_seq = seq // QKV_TILE
    tok = lambda i: (i, 0)
    pos = lambda i: (i % tiles_per_seq, 0)
    return pl.pallas_call(
        _qkv_kernel,
        grid=(m // QKV_TILE,),
        in_specs=[
            pl.BlockSpec((QKV_TILE, D_MODEL), tok),
            _const_spec((1, D_MODEL)),
            _const_spec((D_MODEL, QKV_DIM)),
            pl.BlockSpec((QKV_TILE, LANES), pos),
            pl.BlockSpec((QKV_TILE, LANES), pos),
            pl.BlockSpec((QKV_TILE, LANES), pos),
        ],
        out_specs=[
            pl.BlockSpec((QKV_TILE, Q_DIM), tok),
            pl.BlockSpec((QKV_TILE, KV_DIM), tok),
            pl.BlockSpec((QKV_TILE, KV_DIM), tok),
        ],
        out_shape=[
            jax.ShapeDtypeStruct((m, Q_DIM), _bf16),
            jax.ShapeDtypeStruct((m, KV_DIM), _bf16),
            jax.ShapeDtypeStruct((m, KV_DIM), _bf16),
        ],
        compiler_params=_compiler_params(1),
        name="qkv_rope",
    )(x2, g, w_qkv, cos_t, sin_hi_t, sin_lo_t)


def _spread(thunks, n_slots):
    bounds = [(len(thunks) * i) // n_slots for i in range(n_slots + 1)]
    return [thunks[bounds[i]:bounds[i + 1]] for i in range(n_slots)]


def _run_fillers(group, produced):
    cache = {}

    def zero_row(r):
        r = r // SUBLANES * SUBLANES
        if r not in cache:
            cache[r] = jnp.minimum(jnp.abs(produced[r:r + SUBLANES, :LANES]), 0.0)
        return cache[r][:1, :]

    n_rows = produced.shape[0]
    for i, thunk in enumerate(group):
        thunk(zero_row, (i * n_rows) // len(group), ((i + 1) * n_rows) // len(group))


def _run_ungated(group):
    zero = jnp.zeros((1, LANES), _f32)
    for thunk in group:
        thunk(lambda r: zero, 0, 0)


def _ffn_residual(xf, g, wgu_ref, wd_ref, act_ref, fillers, fillers_have_matmuls):
    n_up = D_FF // LANES
    n_down = D_MODEL // FF_CHUNK
    groups = _spread(list(fillers), n_up + n_down)
    h = _rmsnorm(xf, g).astype(_bf16)
    previous = xf
    for c in range(n_up):
        lo = c * FF_CHUNK
        gate_up = jnp.dot(h, wgu_ref[:, lo:lo + FF_CHUNK], preferred_element_type=_f32)
        gate = gate_up[:, :LANES]
        up = gate_up[:, LANES:]
        act = gate * jax.nn.sigmoid(gate) * up
        act_ref[:, c * LANES:(c + 1) * LANES] = act.astype(_bf16)
        _run_fillers(groups[c], previous if fillers_have_matmuls else act)
        previous = act
    outs = []
    for c in range(n_down):
        lo = c * FF_CHUNK
        out = xf[:, lo:lo + FF_CHUNK] + jnp.dot(act_ref[...], wd_ref[:, lo:lo + FF_CHUNK],
                                                 preferred_element_type=_f32)
        outs.append(out)
        _run_fillers(groups[n_up + c], previous if fillers_have_matmuls else out)
        previous = out
    return jnp.concatenate(outs, axis=1)


def _stage_kv_stacks(kp_ref, kc_ref, kn_ref, vp_ref, vc_ref, vn_ref, kz_ref, vz_ref):
    lane = lax.broadcasted_iota(jnp.int32, (BLK, LANES), 1)
    low_lanes = lane < HEAD_DIM
    zero = jnp.zeros((), _bf16)
    high_flag = (lane // HEAD_DIM).astype(_f32)
    ones_low = (1.0 - high_flag).astype(_bf16)
    ones_high = high_flag.astype(_bf16)
    for kb in range(ATT_KBLKS):
        if kb == 0:
            k_blk, v_blk = kp_ref[...], vp_ref[...]
        elif kb == ATT_KBLKS - 1:
            k_blk, v_blk = kn_ref[...], vn_ref[...]
        else:
            rows = slice((kb - 1) * BLK, kb * BLK)
            k_blk, v_blk = kc_ref[rows, :], vc_ref[rows, :]
        for g in range(N_KV_HEADS):
            tile = slice((g // HEADS_PER_TILE) * LANES, (g // HEADS_PER_TILE + 1) * LANES)
            own = low_lanes if g % HEADS_PER_TILE == 0 else ~low_lanes
            k_own = jnp.where(own, k_blk[:, tile], zero)
            v_own = jnp.where(own, v_blk[:, tile], zero)
            k_swap = pltpu.roll(k_own, HEAD_DIM, 1)
            v_swap = pltpu.roll(v_own, HEAD_DIM, 1)
            if g % HEADS_PER_TILE == 0:
                k_lo, k_hi, v_lo, v_hi = k_own, k_swap, v_own, v_swap
            else:
                k_lo, k_hi, v_lo, v_hi = k_swap, k_own, v_swap, v_own
            base = kb * PAIR_ROWS
            kz_ref[g, base:base + BLK, :] = k_lo
            kz_ref[g, base + BLK:base + PAIR_ROWS, :] = k_hi
            vz_ref[g, base:base + BLK, :LANES] = v_lo
            vz_ref[g, base + BLK:base + PAIR_ROWS, :LANES] = v_hi
            vz_ref[g, base:base + BLK, LANES:] = ones_low
            vz_ref[g, base + BLK:base + PAIR_ROWS, LANES:] = ones_high


def _attn_group_unit(j, g, prev_ok, next_ok, low_lanes_f32, sink_ref, q_ref, kz_ref, vz_ref, o_ref,
                     p_ref, sink_term_ref):
    slot = (j * N_KV_HEADS + g) % ATT_P_SLOTS
    rows = slice(j * BLK, (j + 1) * BLK)
    stack = slice(j * PAIR_ROWS, (j + 3) * PAIR_ROWS)
    q_tiles = [g * PAIRS_PER_GROUP + pair for pair in range(PAIRS_PER_GROUP)]

    def scores(zero_row, row_lo, row_hi):
        del row_hi
        q2 = jnp.concatenate([q_ref[rows, qt * LANES:(qt + 1) * LANES] for qt in q_tiles], axis=0)
        q2 = q2 + zero_row(row_lo).astype(_bf16)
        s_all = lax.dot_general(q2, kz_ref[g, stack, :], (((1,), (1,)), ((), ())),
                                preferred_element_type=_f32)
        for pair, qt in enumerate(q_tiles):
            s = s_all[pair * BLK:(pair + 1) * BLK, :]
            probs = [None] * (3 * HEADS_PER_TILE)
            sink_terms = []
            for half in range(HEADS_PER_TILE):
                sink = sink_ref[qt * HEADS_PER_TILE + half]
                col = lambda kb: slice((kb * HEADS_PER_TILE + half) * BLK, (kb * HEADS_PER_TILE + half + 1) * BLK)
                s_prev = jnp.where(prev_ok, s[:, col(0)], NEG)
                s_own = s[:, col(1)]
                s_next = jnp.where(next_ok, s[:, col(2)], NEG)
                mx = jnp.max(jnp.maximum(jnp.maximum(s_prev, s_own), s_next), axis=-1, keepdims=True)
                mx = jnp.maximum(mx, sink)
                probs[0 * HEADS_PER_TILE + half] = jnp.exp(s_prev - mx).astype(_bf16)
                probs[1 * HEADS_PER_TILE + half] = jnp.exp(s_own - mx).astype(_bf16)
                probs[2 * HEADS_PER_TILE + half] = jnp.exp(s_next - mx).astype(_bf16)
                sink_terms.append(jnp.exp(sink - mx))
            p_ref[slot, pair * BLK:(pair + 1) * BLK, :] = jnp.concatenate(probs, axis=1)
            sink_term_ref[slot, pair * BLK:(pair + 1) * BLK, :] = jnp.where(low_lanes_f32, sink_terms[0],
                                                                            sink_terms[1])

    def values(zero_row, row_lo, row_hi):
        del zero_row, row_lo, row_hi
        o_ext = jnp.dot(p_ref[slot], vz_ref[g, stack, :], preferred_element_type=_f32)
        o_all = o_ext[:, :LANES] / (o_ext[:, LANES:] + sink_term_ref[slot])
        for pair, qt in enumerate(q_tiles):
            o_ref[rows, qt * LANES:(qt + 1) * LANES] = o_all[pair * BLK:(pair + 1) * BLK, :].astype(_bf16)

    return scores, values


def _attn_ffn_kernel(sink_ref, q_ref, kp_ref, kc_ref, kn_ref, vp_ref, vc_ref, vn_ref, xb_ref,
                     wo_ref, g_ref, wgu_ref, wd_ref, gc_ref, w1_ref, b1_ref, out_ref, u_out_ref,
                     kz_ref, vz_ref, o_ref, p_ref, sink_term_ref, act_ref, *, n_tiles, tiles_per_seq):
    s = pl.program_id(0)

    def front_units():
        _stage_kv_stacks(kp_ref, kc_ref, kn_ref, vp_ref, vc_ref, vn_ref, kz_ref, vz_ref)
        has_prev = (s % tiles_per_seq) != 0
        has_next = (s % tiles_per_seq) != tiles_per_seq - 1
        qi = lax.broadcasted_iota(jnp.int32, (BLK, BLK), 0)
        ci = lax.broadcasted_iota(jnp.int32, (BLK, BLK), 1)
        prev_in_window = ci >= qi
        next_in_window = ci <= qi
        low_lanes_f32 = (lax.broadcasted_iota(jnp.int32, (BLK, LANES), 1) & HEAD_DIM) == 0
        halves = []
        for j in range(ATT_QBLKS):
            prev_ok = prev_in_window & has_prev if j == 0 else prev_in_window
            next_ok = next_in_window & has_next if j == ATT_QBLKS - 1 else next_in_window
            for g in range(N_KV_HEADS):
                halves.append(_attn_group_unit(j, g, prev_ok, next_ok, low_lanes_f32, sink_ref, q_ref,
                                               kz_ref, vz_ref, o_ref, p_ref, sink_term_ref))
        units = []
        for i in range(len(halves) + ATT_VALUE_LAG):
            if i < len(halves):
                units.append(halves[i][0])
            if i >= ATT_VALUE_LAG:
                units.append(halves[i - ATT_VALUE_LAG][1])
        return units

    def back_half(units):
        x1 = xb_ref[...] + jnp.dot(o_ref[...], wo_ref[...], preferred_element_type=_f32)
        units = units() if callable(units) else units
        x2 = _ffn_residual(x1, g_ref[...], wgu_ref, wd_ref, act_ref, units, fillers_have_matmuls=True)
        out_ref[...] = x2
        h = _rmsnorm(x2, gc_ref[...]).astype(_bf16)
        for c in range(CONV_CH // FF_CHUNK):
            lo = c * FF_CHUNK
            val = (jnp.dot(h, w1_ref[:, lo:lo + FF_CHUNK], preferred_element_type=_f32)
                   + b1_ref[:, lo:lo + FF_CHUNK])
            gate = (jnp.dot(h, w1_ref[:, CONV_CH + lo:CONV_CH + lo + FF_CHUNK], preferred_element_type=_f32)
                    + b1_ref[:, CONV_CH + lo:CONV_CH + lo + FF_CHUNK])
            u_out_ref[:, lo:lo + FF_CHUNK] = val * jax.nn.sigmoid(gate)

    @pl.when(s == 0)
    def _():
        _run_ungated(front_units())

    @pl.when((s > 0) & (s < n_tiles))
    def _():
        back_half(front_units)

    @pl.when(s == n_tiles)
    def _():
        back_half([])


def _attn_ffn_call(x2, q, k, v, sink, w_o, g, w_gu, w_d, gc, w1, b1, seq):
    m = x2.shape[0]
    n_tiles = m // TOK_TILE
    tiles_per_seq = seq // TOK_TILE
    n_blk = m // BLK
    front = lambda s: jnp.minimum(s, n_tiles - 1)
    back = lambda s: jnp.maximum(s - 1, 0)
    tok_front = lambda s: (front(s), 0)
    tok_back = lambda s: (back(s), 0)
    prev = lambda s: (jnp.maximum(front(s) * ATT_QBLKS - 1, 0), 0)
    nxt = lambda s: (jnp.minimum((front(s) + 1) * ATT_QBLKS, n_blk - 1), 0)
    edge_spec = lambda imap: pl.BlockSpec((BLK, KV_DIM), imap)
    own_spec = pl.BlockSpec((TOK_TILE, KV_DIM), tok_front)
    layer0 = lambda shape: _const_spec((None,) + shape, (0,) * (len(shape) + 1))
    return pl.pallas_call(
        functools.partial(_attn_ffn_kernel, n_tiles=n_tiles, tiles_per_seq=tiles_per_seq),
        grid=(n_tiles + 1,),
        in_specs=[
            pl.BlockSpec(memory_space=pltpu.SMEM),
            pl.BlockSpec((TOK_TILE, Q_DIM), tok_front),
            edge_spec(prev), own_spec, edge_spec(nxt),
            edge_spec(prev), own_spec, edge_spec(nxt),
            pl.BlockSpec((TOK_TILE, D_MODEL), tok_back),
            _const_spec((Q_DIM, D_MODEL)),
            _const_spec((1, D_MODEL)),
            layer0((D_MODEL, 2 * D_FF)),
            layer0((D_FF, D_MODEL)),
            _const_spec((1, D_MODEL)),
            _const_spec((D_MODEL, 2 * CONV_CH)),
            _const_spec((1, 2 * CONV_CH)),
        ],
        out_specs=[
            pl.BlockSpec((TOK_TILE, D_MODEL), tok_back),
            pl.BlockSpec((TOK_TILE, CONV_CH), tok_back),
        ],
        out_shape=[
            jax.ShapeDtypeStruct((m, D_MODEL), _f32),
            jax.ShapeDtypeStruct((m, CONV_CH), _f32),
        ],
        scratch_shapes=[
            pltpu.VMEM((N_KV_HEADS, ATT_KBLKS * PAIR_ROWS, LANES), _bf16),
            pltpu.VMEM((N_KV_HEADS, ATT_KBLKS * PAIR_ROWS, 2 * LANES), _bf16),
            pltpu.VMEM((TOK_TILE, Q_DIM), _bf16),
            pltpu.VMEM((ATT_P_SLOTS, PAIRS_PER_GROUP * BLK, 3 * PAIR_ROWS), _bf16),
            pltpu.VMEM((ATT_P_SLOTS, PAIRS_PER_GROUP * BLK, LANES), _f32),
            pltpu.VMEM((TOK_TILE, D_FF), _bf16),
        ],
        compiler_params=_compiler_params(1),
        name="attn_ffn",
    )(sink, q, k, k, k, v, v, v, x2, w_o, g, w_gu, w_d, gc, w1, b1)


def _conv_ffn_kernel(ua_ref, up_ref, un_ref, xb_ref, wdw_ref, bdw_ref, lng_ref, lnb_ref,
                     w2_ref, b2_ref, gf_ref, wgu_ref, wd_ref, gfin_ref, out_ref,
                     u_ref, c_ref, y_ref, act_ref, *, n_tiles, tiles_per_seq):
    s = pl.program_id(0)

    @pl.when(s == 0)
    def _():
        y_ref[...] = jnp.zeros_like(y_ref)

    t_idx = jnp.minimum(s, n_tiles - 1)
    has_prev = (t_idx % tiles_per_seq) != 0
    has_next = (t_idx % tiles_per_seq) != tiles_per_seq - 1
    u_prev = jnp.where(has_prev, up_ref[...], 0.0)
    u_next = jnp.where(has_next, un_ref[...], 0.0)
    for t in range(N_LANE_TILES):
        tile = slice(t * LANES, (t + 1) * LANES)
        u_ref[t, :HALO, :] = u_prev[:, tile]
        u_ref[t, HALO:HALO + TOK_TILE, :] = ua_ref[:, tile]
        u_ref[t, HALO + TOK_TILE:, :] = u_next[:, tile]

    def conv_unit(t, rc):
        def run(zero_row, row_lo, row_hi):
            acc = None
            for j in range(CONV_WIDTH):
                start = rc * CONV_ROWS + HALO - CONV_PAD + j
                w_rows = wdw_ref[t, j] + zero_row(row_lo + (j * (row_hi - row_lo)) // CONV_WIDTH)
                taps = u_ref[t, pl.ds(start, CONV_ROWS), :].reshape(CONV_ROWS // SUBLANES, SUBLANES, LANES)
                prod = taps * w_rows[None]
                acc = prod if acc is None else acc + prod
            c_ref[t, pl.ds(rc * CONV_ROWS, CONV_ROWS), :] = acc.reshape(CONV_ROWS, LANES) + bdw_ref[t]
        return run

    def norm_unit(rc):
        def run(zero_row, row_lo, row_hi):
            del zero_row, row_lo, row_hi
            rsl = pl.ds(rc * CONV_ROWS, CONV_ROWS)
            c = jnp.concatenate([c_ref[t, rsl, :] for t in range(N_LANE_TILES)], axis=1)
            mu = jnp.mean(c, axis=-1, keepdims=True)
            var = jnp.mean(jnp.square(c - mu), axis=-1, keepdims=True)
            y = (c - mu) * lax.rsqrt(var + EPS) * lng_ref[...] + lnb_ref[...]
            y_ref[rsl, :] = (y * jax.nn.sigmoid(y)).astype(_bf16)
        return run

    n_rc = TOK_TILE // CONV_ROWS
    units = [conv_unit(t, rc) for t in range(N_LANE_TILES) for rc in range(n_rc)]
    units += [norm_unit(rc) for rc in range(n_rc)]

    n_pw2 = D_MODEL // FF_CHUNK
    n_ffn = D_FF // LANES + D_MODEL // FF_CHUNK
    n_early = (len(units) * n_pw2) // (n_pw2 + n_ffn)
    pw2_groups = _spread(units[:n_early], n_pw2)
    x1_cols = []
    for c in range(n_pw2):
        lo = c * FF_CHUNK
        x1_c = (xb_ref[:, lo:lo + FF_CHUNK]
                + jnp.dot(y_ref[...], w2_ref[:, lo:lo + FF_CHUNK], preferred_element_type=_f32)
                + b2_ref[:, lo:lo + FF_CHUNK])
        x1_cols.append(x1_c)
        _run_fillers(pw2_groups[c], x1_c)
    x1 = jnp.concatenate(x1_cols, axis=1)
    x2 = _ffn_residual(x1, gf_ref[...], wgu_ref, wd_ref, act_ref, units[n_early:], fillers_have_matmuls=False)
    out_ref[...] = _rmsnorm(x2, gfin_ref[...])


def _conv_ffn_call(x2, u, wdw, bdw, lng, lnb, w2, b2, gf, w_gu, w_d, gfin, seq):
    m = x2.shape[0]
    n_tiles = m // TOK_TILE
    tiles_per_seq = seq // TOK_TILE
    halo_per_tile = TOK_TILE // HALO
    n_halo = m // HALO
    front = lambda s: jnp.minimum(s, n_tiles - 1)
    back = lambda s: jnp.maximum(s - 1, 0)
    tok_front = lambda s: (front(s), 0)
    tok_back = lambda s: (back(s), 0)
    prev = lambda s: (jnp.maximum(front(s) * halo_per_tile - 1, 0), 0)
    nxt = lambda s: (jnp.minimum((front(s) + 1) * halo_per_tile, n_halo - 1), 0)
    rows = TOK_TILE + 2 * HALO
    layer1 = lambda shape: _const_spec((None,) + shape, (1,) + (0,) * len(shape))
    return pl.pallas_call(
        functools.partial(_conv_ffn_kernel, n_tiles=n_tiles, tiles_per_seq=tiles_per_seq),
        grid=(n_tiles + 1,),
        in_specs=[
            pl.BlockSpec((TOK_TILE, CONV_CH), tok_front),
            pl.BlockSpec((HALO, CONV_CH), prev),
            pl.BlockSpec((HALO, CONV_CH), nxt),
            pl.BlockSpec((TOK_TILE, D_MODEL), tok_back),
            _const_spec((N_LANE_TILES, CONV_WIDTH, SUBLANES, LANES)),
            _const_spec((N_LANE_TILES, 1, LANES)),
            _const_spec((1, CONV_CH)),
            _const_spec((1, CONV_CH)),
            _const_spec((CONV_CH, D_MODEL)),
            _const_spec((1, D_MODEL)),
            _const_spec((1, D_MODEL)),
            layer1((D_MODEL, 2 * D_FF)),
            layer1((D_FF, D_MODEL)),
            _const_spec((1, D_MODEL)),
        ],
        out_specs=pl.BlockSpec((TOK_TILE, D_MODEL), tok_back),
        out_shape=jax.ShapeDtypeStruct((m, D_MODEL), _f32),
        scratch_shapes=[
            pltpu.VMEM((N_LANE_TILES, rows, LANES), _f32),
            pltpu.VMEM((N_LANE_TILES, TOK_TILE, LANES), _f32),
            pltpu.VMEM((TOK_TILE, CONV_CH), _bf16),
            pltpu.VMEM((TOK_TILE, D_FF), _bf16),
        ],
        compiler_params=_compiler_params(1),
        name="conv_ffn",
    )(u, u, u, x2, wdw, bdw, lng, lnb, w2, b2, gf, w_gu, w_d, gfin)


def _rope_tables(seq):
    pos = np.arange(seq, dtype=np.float64)
    inv_freq = ROPE_THETA ** (-np.arange(0, ROT_DIM, 2, dtype=np.float64) / ROT_DIM)
    ang = pos[:, None] * inv_freq[None, :]
    cos = np.cos(ang)
    sin = np.sin(ang)
    ones = np.ones((seq, HEAD_DIM - ROT_DIM))
    zeros_half = np.zeros((seq, ROT_HALF))
    zeros_rest = np.zeros((seq, HEAD_DIM - ROT_DIM))
    cos_h = np.concatenate([cos, cos, ones], axis=1)
    sin_hi_h = np.concatenate([zeros_half, sin, zeros_rest], axis=1)
    sin_lo_h = np.concatenate([-sin, zeros_half, zeros_rest], axis=1)
    rep = lambda a: jnp.asarray(np.tile(a, (1, HEADS_PER_TILE)), dtype=_f32)
    return rep(cos_h), rep(sin_hi_h), rep(sin_lo_h)


def kernel(x, attn_norm, attn_w_qkv, attn_w_o, attn_sink, conv_norm, conv_w_pw1, conv_b_pw1, conv_w_dw,
           conv_b_dw, conv_ln_g, conv_ln_b, conv_w_pw2, conv_b_pw2, ffn_norm, ffn_w_gu, ffn_w_down, final_norm):
    b, s, d = x.shape
    assert d == D_MODEL and s % TOK_TILE == 0 and s % QKV_TILE == 0
    assert attn_norm.shape[0] == 1 and conv_norm.shape[0] == 1 and ffn_norm.shape[0] == 2
    m = b * s
    x2 = x.reshape(m, d)
    row = lambda v: v.reshape(1, -1).astype(_f32)
    bf = lambda w: w.astype(_bf16)
    w_gu = bf(ffn_w_gu).reshape(ffn_w_gu.shape[0], D_MODEL, 2, D_FF // LANES, LANES)
    w_gu = w_gu.transpose(0, 1, 3, 2, 4).reshape(ffn_w_gu.shape[0], D_MODEL, 2 * D_FF)
    w_d = bf(ffn_w_down)

    cos_t, sin_hi_t, sin_lo_t = _rope_tables(s)
    q, k, v = _qkv_call(x2, row(attn_norm[0]), bf(attn_w_qkv[0]), cos_t, sin_hi_t, sin_lo_t, s)
    x2, u = _attn_ffn_call(x2, q, k, v, attn_sink[0].astype(_f32), bf(attn_w_o[0]), row(ffn_norm[0]),
                           w_gu, w_d, row(conv_norm[0]), bf(conv_w_pw1[0]), row(conv_b_pw1[0]), s)

    wdw = conv_w_dw[0].reshape(CONV_WIDTH, N_LANE_TILES, LANES).transpose(1, 0, 2)
    wdw = jnp.broadcast_to(wdw[:, :, None, :], (N_LANE_TILES, CONV_WIDTH, SUBLANES, LANES))
    bdw = conv_b_dw[0].reshape(N_LANE_TILES, 1, LANES)
    out = _conv_ffn_call(x2, u, wdw, bdw, row(conv_ln_g[0]), row(conv_ln_b[0]), bf(conv_w_pw2[0]),
                         row(conv_b_pw2[0]), row(ffn_norm[1]), w_gu, w_d, row(final_norm), s)
    return out.reshape(b, s, d)
```
